```python
import jax, jax.numpy as jnp
from jax import lax
import numpy as np

D_MODEL = 1024
BATCH = 4
SEQ = 4096
DEPTH = 2

N_MIXERS = 2
N_CONV_LAYERS = (DEPTH + 1) // 2
N_SGU_LAYERS = DEPTH // 2
CONV_WIDTH = 3
SGU_WIDTH = 2 * D_MODEL
SGU_HEADS = 8
SGU_HEAD_DIM = SGU_WIDTH // SGU_HEADS
CHUNK = 128
N_EXPERTS = 16
N_GROUPS = 4
EXPERTS_PER_GROUP = N_EXPERTS // N_GROUPS
TOP_K = 2
EXPERT_FF = D_MODEL // 2
DEEPNORM_ALPHA = (2 * DEPTH) ** 0.25
DEEPNORM_BETA = (8 * DEPTH) ** -0.25
LN_EPS = 1e-5

kernel_name = 'hybrid_conv_sgu_grouped_moe_deepnorm'


def layer_norm(x, g, b):
    xf = x.astype(jnp.float32)
    mu = jnp.mean(xf, axis=-1, keepdims=True)
    var = jnp.mean(jnp.square(xf - mu), axis=-1, keepdims=True)
    y = (xf - mu) * lax.rsqrt(var + LN_EPS)
    return (y * g.astype(jnp.float32) + b.astype(jnp.float32)).astype(x.dtype)


def short_conv_mixer(x, w_in, conv_w, w_out):
    d = x.shape[-1]
    bch = jnp.einsum('bsd,de->bse', x, w_in)
    b_gate, c_gate, h = jnp.split(bch, 3, axis=-1)
    z = lax.conv_general_dilated(
        c_gate * h, conv_w[:, None, :], window_strides=(1,),
        padding=[(CONV_WIDTH - 1, 0)],
        dimension_numbers=('NWC', 'WIO', 'NWC'), feature_group_count=d)
    return jnp.einsum('bsd,de->bse', b_gate * z, w_out)


def chunked_sgu_mixer(x, w_in, ln_g, ln_b, w_s, b_s, w_out):
    bn, s, _ = x.shape
    uv = jax.nn.gelu(jnp.einsum('bsd,de->bse', x, w_in), approximate=False)
    u, v = jnp.split(uv, 2, axis=-1)
    v = layer_norm(v, ln_g, ln_b)
    v = v.reshape(bn, s // CHUNK, CHUNK, SGU_HEADS, SGU_HEAD_DIM)
    causal = jnp.tril(jnp.ones((CHUNK, CHUNK), dtype=bool))
    w_causal = jnp.where(causal[None], w_s, 0)
    mixed = jnp.einsum('hij,bnjhc->bnihc', w_causal, v) + b_s.T[:, :, None]
    mixed = mixed.reshape(bn, s, SGU_WIDTH)
    return jnp.einsum('bsf,fd->bsd', u * mixed, w_out)


def grouped_moe(x, router_w, router_bias, w_gate, w_up, w_down):
    bn, s, d = x.shape
    xt = x.reshape(bn * s, d)
    logits = jnp.einsum('td,de->te', xt.astype(jnp.float32), router_w.astype(jnp.float32))
    scores = jax.nn.sigmoid(logits)
    biased = (scores + router_bias.astype(jnp.float32)).reshape(-1, N_GROUPS, EXPERTS_PER_GROUP)
    group_score = lax.top_k(biased, TOP_K)[0].sum(-1)
    group = jnp.argmax(group_score, axis=-1)
    in_group = jnp.take_along_axis(biased, group[:, None, None], axis=1)[:, 0]
    _, local_idx = lax.top_k(in_group, TOP_K)
    expert_idx = group[:, None] * EXPERTS_PER_GROUP + local_idx
    w = jnp.take_along_axis(scores, expert_idx, axis=1)
    w = w / jnp.sum(w, axis=-1, keepdims=True)
    gates = jnp.einsum('tk,tke->te', w, jax.nn.one_hot(expert_idx, N_EXPERTS, dtype=jnp.float32)).astype(x.dtype)
    y = jnp.zeros_like(xt)
    for e in range(N_EXPERTS):
        h = jax.nn.silu(xt @ w_gate[e]) * (xt @ w_up[e])
        y = y + gates[:, e:e + 1] * (h @ w_down[e])
    return y.reshape(bn, s, d)


def setup_inputs(seed: int = 0) -> dict:
    key = jax.random.key(seed)
    ks = jax.random.split(key, 20)
    f32 = jnp.float32
    D, E, F = D_MODEL, N_EXPERTS, EXPERT_FF
    nrm = lambda k, shape, scale: jax.random.normal(k, shape, f32) * scale
    return {
        'x': nrm(ks[0], (BATCH, SEQ, D), 1.0),
        'a_w_in': nrm(ks[1], (N_CONV_LAYERS, D, 3 * D), D ** -0.5),
        'a_conv_w': nrm(ks[2], (N_CONV_LAYERS, CONV_WIDTH, D), CONV_WIDTH ** -0.5),
        'a_w_out': nrm(ks[3], (N_CONV_LAYERS, D, D), D ** -0.5 * DEEPNORM_BETA),
        'b_w_in': nrm(ks[4], (N_SGU_LAYERS, D, 2 * SGU_WIDTH), D ** -0.5),
        'b_ln_g': 1.0 + nrm(ks[5], (N_SGU_LAYERS, SGU_WIDTH), 0.02),
        'b_ln_b': nrm(ks[6], (N_SGU_LAYERS, SGU_WIDTH), 0.02),
        'b_ws': nrm(ks[7], (N_SGU_LAYERS, SGU_HEADS, CHUNK, CHUNK), CHUNK ** -0.5),
        'b_bs': 1.0 + nrm(ks[8], (N_SGU_LAYERS, SGU_HEADS, CHUNK), 0.1),
        'b_w_out': nrm(ks[9], (N_SGU_LAYERS, SGU_WIDTH, D), SGU_WIDTH ** -0.5 * DEEPNORM_BETA),
        'router_w': nrm(ks[10], (D, E), D ** -0.5),
        'router_bias': nrm(ks[11], (E,), 0.01),
        'moe_w_gate': nrm(ks[12], (DEPTH, E, D, F), D ** -0.5),
        'moe_w_up': nrm(ks[13], (DEPTH, E, D, F), D ** -0.5),
        'moe_w_down': nrm(ks[14], (DEPTH, E, F, D), F ** -0.5 * DEEPNORM_BETA),
        'ln_mix_g': 1.0 + nrm(ks[15], (DEPTH, D), 0.02),
        'ln_mix_b': nrm(ks[16], (DEPTH, D), 0.02),
        'ln_ffn_g': 1.0 + nrm(ks[17], (DEPTH, D), 0.02),
        'ln_ffn_b': nrm(ks[18], (DEPTH, D), 0.02),
    }


def reference(x, a_w_in, a_conv_w, a_w_out, b_w_in, b_ln_g, b_ln_b, b_ws, b_bs, b_w_out,
              router_w, router_bias, moe_w_gate, moe_w_up, moe_w_down,
              ln_mix_g, ln_mix_b, ln_ffn_g, ln_ffn_b):
    for i in range(DEPTH):
        j = i // N_MIXERS
        if i % N_MIXERS == 0:
            m = short_conv_mixer(x, a_w_in[j], a_conv_w[j], a_w_out[j])
        else:
            m = chunked_sgu_mixer(x, b_w_in[j], b_ln_g[j], b_ln_b[j], b_ws[j], b_bs[j], b_w_out[j])
        x = layer_norm(DEEPNORM_ALPHA * x + m, ln_mix_g[i], ln_mix_b[i])
        f = grouped_moe(x, router_w, router_bias, moe_w_gate[i], moe_w_up[i], moe_w_down[i])
        x = layer_norm(DEEPNORM_ALPHA * x + f, ln_ffn_g[i], ln_ffn_b[i])
    return x
```

```python
import functools

import jax
import jax.numpy as jnp
from jax import lax
from jax.experimental import pallas as pl
from jax.experimental.pallas import tpu as pltpu

F32 = jnp.float32
BF16 = jnp.bfloat16
HIGHEST = lax.Precision.HIGHEST

N_EXPERTS = 16
N_GROUPS = 4
EXPERTS_PER_GROUP = 4
CHUNK = 128
SGU_HEADS = 8
CONV_WIDTH = 3
DEPTH = 2
DEEPNORM_ALPHA = (2 * DEPTH) ** 0.25
LN_EPS = 1e-5

_PAIRS = ((0, 1), (0, 2), (0, 3), (1, 3), (2, 3), (2, 1))
N_PAIRS = len(_PAIRS)
N_CLASSES = N_GROUPS * N_PAIRS
CLASS_ROWS = 32
_CLASS_EA = tuple(g * EXPERTS_PER_GROUP + p[0] for g in range(N_GROUPS) for p in _PAIRS)
_CLASS_EB = tuple(g * EXPERTS_PER_GROUP + p[1] for g in range(N_GROUPS) for p in _PAIRS)

V7X_VMEM_LIMIT = 56 * 1024 * 1024


def _layer_norm(y, g, b):
    mu = jnp.mean(y, axis=-1, keepdims=True)
    d = y - mu
    var = jnp.mean(d * d, axis=-1, keepdims=True)
    return d * lax.rsqrt(var + LN_EPS) * g + b


def _gelu(x):
    return 0.5 * x * (1.0 + lax.erf(x * (2.0 ** -0.5)))


def _route_class(x1, rwt, rb):
    logits = lax.dot_general(rwt, x1, (((1,), (1,)), ((), ())), precision=HIGHEST,
                             preferred_element_type=F32)
    biased = jax.nn.sigmoid(logits) + rb
    v = [biased[e:e + 1, :] for e in range(N_EXPERTS)]
    best_score = None
    best_cls = None
    for g in range(N_GROUPS):
        vg = v[g * EXPERTS_PER_GROUP:(g + 1) * EXPERTS_PER_GROUP]
        sel = []
        for i in range(EXPERTS_PER_GROUP):
            rank = jnp.zeros_like(vg[i], dtype=jnp.int32)
            for j in range(EXPERTS_PER_GROUP):
                if j == i:
                    continue
                beats = (vg[j] >= vg[i]) if j < i else (vg[j] > vg[i])
                rank = rank + beats.astype(jnp.int32)
            sel.append(rank < 2)
        top_sum = None
        cls_g = jnp.zeros_like(vg[0], dtype=jnp.int32)
        for p, (a, b) in enumerate(_PAIRS):
            is_pair = sel[a] & sel[b]
            cls_g = jnp.where(is_pair, g * N_PAIRS + p, cls_g)
            pair_sum = vg[min(a, b)] + vg[max(a, b)]
            top_sum = jnp.where(is_pair, pair_sum, 0.0 if top_sum is None else top_sum)
        if best_score is None:
            best_score, best_cls = top_sum, cls_g
        else:
            better = top_sum > best_score
            best_score = jnp.where(better, top_sum, best_score)
            best_cls = jnp.where(better, cls_g, best_cls)
    return best_cls


def _conv_mixer_kernel(x_ref, w_in_ref, cw_ref, w_out_ref, g_ref, b_ref, rwt_ref, rb_ref,
                       x1_ref, cls_ref, carry_ref):
    ts, d = x_ref.shape

    @pl.when(pl.program_id(1) == 0)
    def _():
        carry_ref[...] = jnp.zeros_like(carry_ref)

    x = x_ref[...]
    xb = x.astype(BF16)
    cg = jnp.dot(xb, w_in_ref[:, d:2 * d], preferred_element_type=F32)
    h = jnp.dot(xb, w_in_ref[:, 2 * d:3 * d], preferred_element_type=F32)
    ch = cg * h
    prev2 = carry_ref[6:7, :]
    prev1 = carry_ref[7:8, :]
    rows = lax.broadcasted_iota(jnp.int32, (ts, 1), 0)
    ch1 = jnp.where(rows == 0, prev1, pltpu.roll(ch, 1, 0))
    ch2 = jnp.where(rows == 0, prev2, jnp.where(rows == 1, prev1, pltpu.roll(ch, 2, 0)))
    z = cw_ref[0:1, :] * ch2 + cw_ref[1:2, :] * ch1 + cw_ref[2:3, :] * ch
    carry_ref[...] = ch[ts - 8:ts, :]
    bg = jnp.dot(xb, w_in_ref[:, 0:d], preferred_element_type=F32)
    m = jnp.dot((bg * z).astype(BF16), w_out_ref[...], preferred_element_type=F32)
    x1 = _layer_norm(DEEPNORM_ALPHA * x + m, g_ref[...], b_ref[...])
    x1_ref[...] = x1
    cls_ref[...] = _route_class(x1, rwt_ref[...], rb_ref[...])


def _conv_mixer(x, w_in, conv_w, w_out, ln_g, ln_b, rwt, rb, ts):
    bn, s, d = x.shape
    ns = s // ts
    const = lambda shape: pl.BlockSpec(shape, lambda b, i: (0,) * len(shape))
    return pl.pallas_call(
        _conv_mixer_kernel,
        grid=(bn, ns),
        in_specs=[
            pl.BlockSpec((None, ts, d), lambda b, i: (b, i, 0)),
            const((d, 3 * d)), const((CONV_WIDTH, d)), const((d, d)),
            const((1, d)), const((1, d)), const((N_EXPERTS, d)), const((N_EXPERTS, 1)),
        ],
        out_specs=[
            pl.BlockSpec((ts, d), lambda b, i: (b * ns + i, 0)),
            pl.BlockSpec((1, ts), lambda b, i: (0, b * ns + i)),
        ],
        out_shape=[
            jax.ShapeDtypeStruct((bn * s, d), F32),
            jax.ShapeDtypeStruct((1, bn * s), jnp.int32),
        ],
        scratch_shapes=[pltpu.VMEM((8, d), F32)],
        compiler_params=pltpu.CompilerParams(
            dimension_semantics=("arbitrary", "arbitrary"), vmem_limit_bytes=V7X_VMEM_LIMIT),
        name="conv_mixer",
    )(x, w_in, conv_w, w_out, ln_g, ln_b, rwt, rb)


def _row_copy(src_hbm, buf, sem, slot, src_row, dst_row):
    return pltpu.make_async_copy(src_hbm.at[pl.ds(src_row, 1), :],
                                 buf.at[slot, pl.ds(dst_row, 1), :], sem.at[slot])


def _start_gather(idx_ref, src_hbm, buf, sem, slot, tile, rows):
    base = tile * rows

    def body(r, carry):
        _row_copy(src_hbm, buf, sem, slot, idx_ref[base + r], r).start()
        return carry

    lax.fori_loop(0, rows, body, 0)


def _wait_gather(src_hbm, buf, sem, slot, rows):
    pltpu.make_async_copy(src_hbm.at[pl.ds(0, rows), :], buf.at[slot], sem.at[slot]).wait()


def _sgu_mixer_kernel(pos_ref, xs_hbm, w_in_ref, lng_ref, lnb_ref, ws_ref, bst_ref, w_out_ref,
                      g_ref, b_ref, rwt_ref, rb_ref, x1_ref, cls_ref, buf, sem, v_ref, gate_ref):
    i = pl.program_id(0)
    n = pl.num_programs(0)
    ts, d = x1_ref.shape
    width = v_ref.shape[1]
    hd = width // SGU_HEADS
    slot = lax.rem(i, 2)

    @pl.when(i == 0)
    def _():
        _start_gather(pos_ref, xs_hbm, buf, sem, 0, 0, ts)

    @pl.when(i + 1 < n)
    def _():
        _start_gather(pos_ref, xs_hbm, buf, sem, 1 - slot, i + 1, ts)

    _wait_gather(xs_hbm, buf, sem, slot, ts)
    x = buf[slot]
    xb = x.astype(BF16)
    v = _gelu(jnp.dot(xb, w_in_ref[:, width:2 * width], preferred_element_type=F32))
    v_ref[...] = _layer_norm(v, lng_ref[...], lnb_ref[...]).astype(BF16)
    r_i = lax.broadcasted_iota(jnp.int32, (CHUNK, CHUNK), 0)
    c_i = lax.broadcasted_iota(jnp.int32, (CHUNK, CHUNK), 1)
    causal = r_i >= c_i
    for h in range(SGU_HEADS):
        cols = slice(h * hd, (h + 1) * hd)
        u_h = _gelu(jnp.dot(xb, w_in_ref[:, cols], preferred_element_type=F32))
        w_h = jnp.where(causal, ws_ref[h], 0.0).astype(BF16)
        bias = bst_ref[:, h:h + 1]
        for c in range(ts // CHUNK):
            rws = slice(c * CHUNK, (c + 1) * CHUNK)
            mixed = jnp.dot(w_h, v_ref[rws, cols], preferred_element_type=F32) + bias
            gate_ref[rws, cols] = (u_h[rws, :] * mixed).astype(BF16)
    m = jnp.dot(gate_ref[...], w_out_ref[...], preferred_element_type=F32)
    x1 = _layer_norm(DEEPNORM_ALPHA * x + m, g_ref[...], b_ref[...])
    x1_ref[...] = x1
    cls_ref[...] = _route_class(x1, rwt_ref[...], rb_ref[...])


def _sgu_mixer(pos, xs, w_in, ln_g, ln_b, ws, bst, w_out, g, b, rwt, rb, ts):
    t = pos.shape[0]
    d = xs.shape[1]
    width = w_out.shape[0]
    const = lambda shape: pl.BlockSpec(shape, lambda i, p: (0,) * len(shape))
    return pl.pallas_call(
        _sgu_mixer_kernel,
        grid_spec=pltpu.PrefetchScalarGridSpec(
            num_scalar_prefetch=1,
            grid=(t // ts,),
            in_specs=[
                pl.BlockSpec(memory_space=pl.ANY),
                const((d, 2 * width)), const((1, width)), const((1, width)),
                const((SGU_HEADS, CHUNK, CHUNK)), const((CHUNK, SGU_HEADS)), const((width, d)),
                const((1, d)), const((1, d)), const((N_EXPERTS, d)), const((N_EXPERTS, 1)),
            ],
            out_specs=[
                pl.BlockSpec((ts, d), lambda i, p: (i, 0)),
                pl.BlockSpec((1, ts), lambda i, p: (0, i)),
            ],
            scratch_shapes=[
                pltpu.VMEM((2, ts, d), F32), pltpu.SemaphoreType.DMA((2,)),
                pltpu.VMEM((ts, width), BF16), pltpu.VMEM((ts, width), BF16),
            ],
        ),
        out_shape=[
            jax.ShapeDtypeStruct((t, d), F32),
            jax.ShapeDtypeStruct((1, t), jnp.int32),
        ],
        compiler_params=pltpu.CompilerParams(
            dimension_semantics=("arbitrary",), vmem_limit_bytes=V7X_VMEM_LIMIT),
        name="sgu_mixer",
    )(pos, xs, w_in, ln_g, ln_b, ws, bst, w_out, g, b, rwt, rb)


def _positions_kernel(cls_ref, pos_ref, tile_ea_ref, tile_eb_ref, nused_ref, incl_ref, *, tm):
    t = cls_ref.shape[1]
    ntp = tile_ea_ref.shape[1]
    lanes = 256
    cls = cls_ref[...]
    crow = lax.broadcasted_iota(jnp.int32, (CLASS_ROWS, t), 0)
    onehot = (crow == cls).astype(F32)
    k_i = lax.broadcasted_iota(jnp.int32, (lanes, lanes), 0)
    j_i = lax.broadcasted_iota(jnp.int32, (lanes, lanes), 1)
    upper = (k_i <= j_i).astype(BF16)
    carry = jnp.zeros((CLASS_ROWS, 1), F32)
    for c in range(t // lanes):
        sl = slice(c * lanes, (c + 1) * lanes)
        inc = jnp.dot(onehot[:, sl].astype(BF16), upper, preferred_element_type=F32) + carry
        incl_ref[:, sl] = inc
        carry = inc[:, lanes - 1:lanes]
    rank = jnp.sum(onehot * incl_ref[...], axis=0, keepdims=True) - 1.0
    ntile = jnp.floor((carry + (tm - 1)) * (1.0 / tm))
    srow = lax.broadcasted_iota(jnp.int32, (CLASS_ROWS, 1), 0)
    tstart = jnp.zeros((CLASS_ROWS, 1), F32)
    for c in range(N_CLASSES):
        tstart = tstart + jnp.where(srow > c, ntile[c:c + 1, :], 0.0)
    tend = tstart + ntile
    pos = jnp.sum(onehot * (tstart * tm), axis=0, keepdims=True) + rank
    pos_ref[...] = pos.astype(jnp.int32)
    nused = jnp.max(jnp.where(srow < N_CLASSES, tend, 0.0), axis=0, keepdims=True)
    last_cls = jnp.max(jnp.where((ntile > 0) & (srow < N_CLASSES), srow, 0), axis=0, keepdims=True)
    tile_i = lax.broadcasted_iota(jnp.int32, (CLASS_ROWS, ntp), 1).astype(F32)
    done = ((tile_i >= tend) & (lax.broadcasted_iota(jnp.int32, (CLASS_ROWS, ntp), 0) < N_CLASSES))
    tcls = jnp.minimum(jnp.sum(done.astype(jnp.int32), axis=0, keepdims=True), last_cls)
    ea = jnp.zeros((1, ntp), jnp.int32)
    eb = jnp.zeros((1, ntp), jnp.int32)
    for c in range(N_CLASSES):
        ea = jnp.where(tcls == c, _CLASS_EA[c], ea)
        eb = jnp.where(tcls == c, _CLASS_EB[c], eb)
    tile_ea_ref[...] = ea
    tile_eb_ref[...] = eb
    nused_ref[...] = jnp.broadcast_to(nused, nused_ref.shape).astype(jnp.int32)


def _positions(cls, tm, ntp):
    t = cls.shape[1]
    return pl.pallas_call(
        functools.partial(_positions_kernel, tm=tm),
        out_shape=[
            jax.ShapeDtypeStruct((1, t), jnp.int32),
            jax.ShapeDtypeStruct((1, ntp), jnp.int32),
            jax.ShapeDtypeStruct((1, ntp), jnp.int32),
            jax.ShapeDtypeStruct((1, 128), jnp.int32),
        ],
        scratch_shapes=[pltpu.VMEM((CLASS_ROWS, t), F32)],
        compiler_params=pltpu.CompilerParams(vmem_limit_bytes=V7X_VMEM_LIMIT),
        name="positions",
    )(cls)


def _invert_kernel(pos_ref, tok_ref):
    t = pos_ref.shape[0]
    r = tok_ref.shape[0]

    def fill(k, carry):
        tok_ref[k] = 0
        return carry

    lax.fori_loop(0, r, fill, 0)

    def put(k, carry):
        tok_ref[pos_ref[k]] = k
        return carry

    lax.fori_loop(0, t, put, 0)


def _invert(pos, r):
    return pl.pallas_call(
        _invert_kernel,
        in_specs=[pl.BlockSpec(memory_space=pltpu.SMEM)],
        out_specs=pl.BlockSpec(memory_space=pltpu.SMEM),
        out_shape=jax.ShapeDtypeStruct((r,), jnp.int32),
        name="invert_positions",
    )(pos)


def _moe_kernel(tok_ref, ea_ref, eb_ref, nused_ref, x_hbm, rw_ref, wga_ref, wua_ref, wda_ref,
                wgb_ref, wub_ref, wdb_ref, g_ref, b_ref, out_ref, buf, sem):
    i = pl.program_id(0)
    tm, d = out_ref.shape
    nused = nused_ref[0]
    slot = lax.rem(i, 2)

    @pl.when((i == 0) & (nused > 0))
    def _():
        _start_gather(tok_ref, x_hbm, buf, sem, 0, 0, tm)

    @pl.when(i + 1 < nused)
    def _():
        _start_gather(tok_ref, x_hbm, buf, sem, 1 - slot, i + 1, tm)

    @pl.when(i < nused)
    def _():
        _wait_gather(x_hbm, buf, sem, slot, tm)
        x = buf[slot]
        xb = x.astype(BF16)
        scores = jax.nn.sigmoid(jnp.dot(x, rw_ref[...], precision=HIGHEST,
                                        preferred_element_type=F32))
        lane = lax.broadcasted_iota(jnp.int32, scores.shape, 1)
        sa = jnp.sum(jnp.where(lane == ea_ref[i], scores, 0.0), axis=-1, keepdims=True)
        sb = jnp.sum(jnp.where(lane == eb_ref[i], scores, 0.0), axis=-1, keepdims=True)
        denom = sa + sb

        def expert(wg_ref, wu_ref, wd_ref):
            gt = jnp.dot(xb, wg_ref[...], preferred_element_type=F32)
            up = jnp.dot(xb, wu_ref[...], preferred_element_type=F32)
            hid = (jax.nn.silu(gt) * up).astype(BF16)
            return jnp.dot(hid, wd_ref[...], preferred_element_type=F32)

        f = (sa / denom) * expert(wga_ref, wua_ref, wda_ref)
        f = f + (sb / denom) * expert(wgb_ref, wub_ref, wdb_ref)
        out_ref[...] = _layer_norm(DEEPNORM_ALPHA * x + f, g_ref[...], b_ref[...])

    @pl.when(i >= nused)
    def _():
        out_ref[...] = jnp.zeros_like(out_ref)


def _moe(tok, tile_ea, tile_eb, nused, x1, rw_pad, wg, wu, wd, g, b, layer, tm):
    nt = tile_ea.shape[0]
    d = x1.shape[1]
    ff = wg.shape[-1]
    const = lambda shape: pl.BlockSpec(shape, lambda i, *_: (0,) * len(shape))
    up_a = pl.BlockSpec((None, None, d, ff), lambda i, tok, ea, eb, nu: (layer, ea[i], 0, 0))
    up_b = pl.BlockSpec((None, None, d, ff), lambda i, tok, ea, eb, nu: (layer, eb[i], 0, 0))
    dn_a = pl.BlockSpec((None, None, ff, d), lambda i, tok, ea, eb, nu: (layer, ea[i], 0, 0))
    dn_b = pl.BlockSpec((None, None, ff, d), lambda i, tok, ea, eb, nu: (layer, eb[i], 0, 0))
    return pl.pallas_call(
        _moe_kernel,
        grid_spec=pltpu.PrefetchScalarGridSpec(
            num_scalar_prefetch=4,
            grid=(nt,),
            in_specs=[
                pl.BlockSpec(memory_space=pl.ANY), const((d, 128)),
                up_a, up_a, dn_a, up_b, up_b, dn_b,
                const((1, d)), const((1, d)),
            ],
            out_specs=pl.BlockSpec((tm, d), lambda i, *_: (i, 0)),
            scratch_shapes=[pltpu.VMEM((2, tm, d), F32), pltpu.SemaphoreType.DMA((2,))],
        ),
        out_shape=jax.ShapeDtypeStruct((nt * tm, d), F32),
        compiler_params=pltpu.CompilerParams(
            dimension_semantics=("arbitrary",), vmem_limit_bytes=V7X_VMEM_LIMIT),
        name=f"moe_{layer}",
    )(tok, tile_ea, tile_eb, nused, x1, rw_pad, wg, wu, wd, wg, wu, wd, g, b)


def _unsort_kernel(pos_ref, xs_hbm, out_ref, buf, sem):
    i = pl.program_id(0)
    n = pl.num_programs(0)
    ts = out_ref.shape[0]
    slot = lax.rem(i, 2)

    @pl.when(i == 0)
    def _():
        _start_gather(pos_ref, xs_hbm, buf, sem, 0, 0, ts)

    @pl.when(i + 1 < n)
    def _():
        _start_gather(pos_ref, xs_hbm, buf, sem, 1 - slot, i + 1, ts)

    _wait_gather(xs_hbm, buf, sem, slot, ts)
    out_ref[...] = buf[slot]


def _unsort(pos, xs, ts):
    t = pos.shape[0]
    d = xs.shape[1]
    return pl.pallas_call(
        _unsort_kernel,
        grid_spec=pltpu.PrefetchScalarGridSpec(
            num_scalar_prefetch=1,
            grid=(t // ts,),
            in_specs=[pl.BlockSpec(memory_space=pl.ANY)],
            out_specs=pl.BlockSpec((ts, d), lambda i, p: (i, 0)),
            scratch_shapes=[pltpu.VMEM((2, ts, d), F32), pltpu.SemaphoreType.DMA((2,))],
        ),
        out_shape=jax.ShapeDtypeStruct((t, d), F32),
        compiler_params=pltpu.CompilerParams(dimension_semantics=("arbitrary",)),
        name="unsort",
    )(pos, xs)


def _route_and_experts(x1, cls, rw_pad, wg, wu, wd, g, b, layer, tm):
    t = x1.shape[0]
    nt = t // tm + N_CLASSES
    ntp = -(-nt // 128) * 128
    pos, tile_ea, tile_eb, nused = _positions(cls, tm, ntp)
    pos = pos.reshape(t)
    tok = _invert(pos, nt * tm)
    xs = _moe(tok, tile_ea.reshape(ntp)[:nt], tile_eb.reshape(ntp)[:nt], nused.reshape(128)[:1],
              x1, rw_pad, wg, wu, wd, g, b, layer, tm)
    return pos, xs


def _forward(x, a_w_in, a_conv_w, a_w_out, b_w_in, b_ln_g, b_ln_b, b_ws, b_bs, b_w_out,
             router_w, router_bias, moe_w_gate, moe_w_up, moe_w_down,
             ln_mix_g, ln_mix_b, ln_ffn_g, ln_ffn_b, *, ts_conv, ts_sgu, tm, ts_out):
    bn, s, d = x.shape
    rwt = router_w.T.astype(F32)
    rb = router_bias.astype(F32).reshape(N_EXPERTS, 1)
    rw_pad = jnp.pad(router_w.astype(F32), ((0, 0), (0, 128 - N_EXPERTS)))
    wg = moe_w_gate.astype(BF16)
    wu = moe_w_up.astype(BF16)
    wd = moe_w_down.astype(BF16)
    row = lambda a: a.reshape(1, -1)

    x1, cls = _conv_mixer(x, a_w_in[0].astype(BF16), a_conv_w[0], a_w_out[0].astype(BF16),
                          row(ln_mix_g[0]), row(ln_mix_b[0]), rwt, rb, ts_conv)
    pos, xs = _route_and_experts(x1, cls, rw_pad, wg, wu, wd,
                                 row(ln_ffn_g[0]), row(ln_ffn_b[0]), 0, tm)
    x1, cls = _sgu_mixer(pos, xs, b_w_in[0].astype(BF16), row(b_ln_g[0]), row(b_ln_b[0]),
                         b_ws[0], b_bs[0].T, b_w_out[0].astype(BF16),
                         row(ln_mix_g[1]), row(ln_mix_b[1]), rwt, rb, ts_sgu)
    pos, xs = _route_and_experts(x1, cls, rw_pad, wg, wu, wd,
                                 row(ln_ffn_g[1]), row(ln_ffn_b[1]), 1, tm)
    return _unsort(pos, xs, ts_out).reshape(bn, s, d)


def kernel(x, a_w_in, a_conv_w, a_w_out, b_w_in, b_ln_g, b_ln_b, b_ws, b_bs, b_w_out, router_w, router_bias, moe_w_gate, moe_w_up, moe_w_down, ln_mix_g, ln_mix_b, ln_ffn_g, ln_ffn_b):
    return _forward(x, a_w_in, a_conv_w, a_w_out, b_w_in, b_ln_g, b_ln_b, b_ws, b_bs, b_w_out,
                    router_w, router_bias, moe_w_gate, moe_w_up, moe_w_down,
                    ln_mix_g, ln_mix_b, ln_ffn_g, ln_ffn_b,
                    ts_conv=512, ts_sgu=256, tm=256, ts_out=512)
```

```python
import functools

import jax
import jax.numpy as jnp
from jax import lax
from jax.experimental import pallas as pl
from jax.experimental.pallas import tpu as pltpu

F32 = jnp.float32
BF16 = jnp.bfloat16
HIGHEST = lax.Precision.HIGHEST

N_EXPERTS = 16
N_GROUPS = 4
EXPERTS_PER_GROUP = 4
CHUNK = 128
SGU_HEADS = 8
CONV_WIDTH = 3
DEPTH = 2
DEEPNORM_ALPHA = (2 * DEPTH) ** 0.25
LN_EPS = 1e-5

_PAIRS = ((0, 1), (0, 2), (0, 3), (1, 3), (2, 3), (2, 1))
N_PAIRS = len(_PAIRS)
N_CLASSES = N_GROUPS * N_PAIRS
CLASS_ROWS = 32
_CLASS_EA = tuple(g * EXPERTS_PER_GROUP + p[0] for g in range(N_GROUPS) for p in _PAIRS)
_CLASS_EB = tuple(g * EXPERTS_PER_GROUP + p[1] for g in range(N_GROUPS) for p in _PAIRS)

V7X_VMEM_LIMIT = 56 * 1024 * 1024
GATHER_UNROLL = 16
INVERT_UNROLL = 16


def _layer_norm(y, g, b):
    mu = jnp.mean(y, axis=-1, keepdims=True)
    d = y - mu
    var = jnp.mean(d * d, axis=-1, keepdims=True)
    return d * lax.rsqrt(var + LN_EPS) * g + b


def _gelu(x):
    return 0.5 * x * (1.0 + lax.erf(x * (2.0 ** -0.5)))


def _route_class(x1, rwt, rb):
    logits = lax.dot_general(rwt, x1, (((1,), (1,)), ((), ())), precision=HIGHEST,
                             preferred_element_type=F32)
    biased = jax.nn.sigmoid(logits) + rb
    v = [biased[e:e + 1, :] for e in range(N_EXPERTS)]
    best_score = None
    best_cls = None
    for g in range(N_GROUPS):
        vg = v[g * EXPERTS_PER_GROUP:(g + 1) * EXPERTS_PER_GROUP]
        sel = []
        for i in range(EXPERTS_PER_GROUP):
            rank = jnp.zeros_like(vg[i], dtype=jnp.int32)
            for j in range(EXPERTS_PER_GROUP):
                if j == i:
                    continue
                beats = (vg[j] >= vg[i]) if j < i else (vg[j] > vg[i])
                rank = rank + beats.astype(jnp.int32)
            sel.append(rank < 2)
        top_sum = None
        cls_g = jnp.zeros_like(vg[0], dtype=jnp.int32)
        for p, (a, b) in enumerate(_PAIRS):
            is_pair = sel[a] & sel[b]
            cls_g = jnp.where(is_pair, g * N_PAIRS + p, cls_g)
            pair_sum = vg[min(a, b)] + vg[max(a, b)]
            top_sum = jnp.where(is_pair, pair_sum, 0.0 if top_sum is None else top_sum)
        if best_score is None:
            best_score, best_cls = top_sum, cls_g
        else:
            better = top_sum > best_score
            best_score = jnp.where(better, top_sum, best_score)
            best_cls = jnp.where(better, cls_g, best_cls)
    return best_cls


def _conv_mixer_kernel(x_ref, w_in_ref, cw_ref, w_out_ref, g_ref, b_ref, rwt_ref, rb_ref,
                       x1_ref, cls_ref, carry_ref):
    ts, d = x_ref.shape

    @pl.when(pl.program_id(1) == 0)
    def _():
        carry_ref[...] = jnp.zeros_like(carry_ref)

    x = x_ref[...]
    xb = x.astype(BF16)
    cg = jnp.dot(xb, w_in_ref[:, d:2 * d], preferred_element_type=F32)
    h = jnp.dot(xb, w_in_ref[:, 2 * d:3 * d], preferred_element_type=F32)
    ch = cg * h
    prev2 = carry_ref[6:7, :]
    prev1 = carry_ref[7:8, :]
    rows = lax.broadcasted_iota(jnp.int32, (ts, 1), 0)
    ch1 = jnp.where(rows == 0, prev1, pltpu.roll(ch, 1, 0))
    ch2 = jnp.where(rows == 0, prev2, jnp.where(rows == 1, prev1, pltpu.roll(ch, 2, 0)))
    z = cw_ref[0:1, :] * ch2 + cw_ref[1:2, :] * ch1 + cw_ref[2:3, :] * ch
    carry_ref[...] = ch[ts - 8:ts, :]
    bg = jnp.dot(xb, w_in_ref[:, 0:d], preferred_element_type=F32)
    m = jnp.dot((bg * z).astype(BF16), w_out_ref[...], preferred_element_type=F32)
    x1 = _layer_norm(DEEPNORM_ALPHA * x + m, g_ref[...], b_ref[...])
    x1_ref[...] = x1
    cls_ref[...] = _route_class(x1, rwt_ref[...], rb_ref[...])


def _conv_mixer(x, w_in, conv_w, w_out, ln_g, ln_b, rwt, rb, ts):
    bn, s, d = x.shape
    ns = s // ts
    const = lambda shape: pl.BlockSpec(shape, lambda b, i: (0,) * len(shape))
    return pl.pallas_call(
        _conv_mixer_kernel,
        grid=(bn, ns),
        in_specs=[
            pl.BlockSpec((None, ts, d), lambda b, i: (b, i, 0)),
            const((d, 3 * d)), const((CONV_WIDTH, d)), const((d, d)),
            const((1, d)), const((1, d)), const((N_EXPERTS, d)), const((N_EXPERTS, 1)),
        ],
        out_specs=[
            pl.BlockSpec((ts, d), lambda b, i: (b * ns + i, 0)),
            pl.BlockSpec((1, ts), lambda b, i: (0, b * ns + i)),
        ],
        out_shape=[
            jax.ShapeDtypeStruct((bn * s, d), F32),
            jax.ShapeDtypeStruct((1, bn * s), jnp.int32),
        ],
        scratch_shapes=[pltpu.VMEM((8, d), F32)],
        compiler_params=pltpu.CompilerParams(
            dimension_semantics=("arbitrary", "arbitrary"), vmem_limit_bytes=V7X_VMEM_LIMIT),
        name="conv_mixer",
    )(x, w_in, conv_w, w_out, ln_g, ln_b, rwt, rb)


def _row_copy(src_hbm, buf, sem, slot, src_row, dst_row):
    return pltpu.make_async_copy(src_hbm.at[pl.ds(src_row, 1), :],
                                 buf.at[slot, pl.ds(dst_row, 1), :], sem.at[slot])


def _start_gather(idx_ref, src_hbm, buf, sem, slot, tile, rows):
    base = tile * rows

    def body(j, carry):
        r0 = j * GATHER_UNROLL
        for u in range(GATHER_UNROLL):
            _row_copy(src_hbm, buf, sem, slot, idx_ref[base + r0 + u], r0 + u).start()
        return carry

    lax.fori_loop(0, rows // GATHER_UNROLL, body, 0)


def _wait_gather(src_hbm, buf, sem, slot, rows):
    pltpu.make_async_copy(src_hbm.at[pl.ds(0, rows), :], buf.at[slot], sem.at[slot]).wait()


def _sgu_mixer_kernel(pos_ref, xs_hbm, w_in_ref, lng_ref, lnb_ref, ws_ref, bst_ref, w_out_ref,
                      g_ref, b_ref, rwt_ref, rb_ref, x1_ref, cls_ref, buf, sem, v_ref, gate_ref):
    i = pl.program_id(0)
    n = pl.num_programs(0)
    ts, d = x1_ref.shape
    width = v_ref.shape[1]
    hd = width // SGU_HEADS
    slot = lax.rem(i, 2)

    @pl.when(i == 0)
    def _():
        _start_gather(pos_ref, xs_hbm, buf, sem, 0, 0, ts)

    @pl.when(i + 1 < n)
    def _():
        _start_gather(pos_ref, xs_hbm, buf, sem, 1 - slot, i + 1, ts)

    _wait_gather(xs_hbm, buf, sem, slot, ts)
    x = buf[slot]
    xb = x.astype(BF16)
    v = _gelu(jnp.dot(xb, w_in_ref[:, width:2 * width], preferred_element_type=F32))
    v_ref[...] = _layer_norm(v, lng_ref[...], lnb_ref[...]).astype(BF16)
    r_i = lax.broadcasted_iota(jnp.int32, (CHUNK, CHUNK), 0)
    c_i = lax.broadcasted_iota(jnp.int32, (CHUNK, CHUNK), 1)
    causal = r_i >= c_i
    for h in range(SGU_HEADS):
        cols = slice(h * hd, (h + 1) * hd)
        u_h = _gelu(jnp.dot(xb, w_in_ref[:, cols], preferred_element_type=F32))
        w_h = jnp.where(causal, ws_ref[h], 0.0).astype(BF16)
        bias = bst_ref[:, h:h + 1]
        for c in range(ts // CHUNK):
            rws = slice(c * CHUNK, (c + 1) * CHUNK)
            mixed = jnp.dot(w_h, v_ref[rws, cols], preferred_element_type=F32) + bias
            gate_ref[rws, cols] = (u_h[rws, :] * mixed).astype(BF16)
    m = jnp.dot(gate_ref[...], w_out_ref[...], preferred_element_type=F32)
    x1 = _layer_norm(DEEPNORM_ALPHA * x + m, g_ref[...], b_ref[...])
    x1_ref[...] = x1
    cls_ref[...] = _route_class(x1, rwt_ref[...], rb_ref[...])


def _sgu_mixer(pos, xs, w_in, ln_g, ln_b, ws, bst, w_out, g, b, rwt, rb, ts):
    t = pos.shape[0]
    d = xs.shape[1]
    width = w_out.shape[0]
    const = lambda shape: pl.BlockSpec(shape, lambda i, p: (0,) * len(shape))
    return pl.pallas_call(
        _sgu_mixer_kernel,
        grid_spec=pltpu.PrefetchScalarGridSpec(
            num_scalar_prefetch=1,
            grid=(t // ts,),
            in_specs=[
                pl.BlockSpec(memory_space=pl.ANY),
                const((d, 2 * width)), const((1, width)), const((1, width)),
                const((SGU_HEADS, CHUNK, CHUNK)), const((CHUNK, SGU_HEADS)), const((width, d)),
                const((1, d)), const((1, d)), const((N_EXPERTS, d)), const((N_EXPERTS, 1)),
            ],
            out_specs=[
                pl.BlockSpec((ts, d), lambda i, p: (i, 0)),
                pl.BlockSpec((1, ts), lambda i, p: (0, i)),
            ],
            scratch_shapes=[
                pltpu.VMEM((2, ts, d), F32), pltpu.SemaphoreType.DMA((2,)),
                pltpu.VMEM((ts, width), BF16), pltpu.VMEM((ts, width), BF16),
            ],
        ),
        out_shape=[
            jax.ShapeDtypeStruct((t, d), F32),
            jax.ShapeDtypeStruct((1, t), jnp.int32),
        ],
        compiler_params=pltpu.CompilerParams(
            dimension_semantics=("arbitrary",), vmem_limit_bytes=V7X_VMEM_LIMIT),
        name="sgu_mixer",
    )(pos, xs, w_in, ln_g, ln_b, ws, bst, w_out, g, b, rwt, rb)


def _positions_kernel(cls_ref, pos_ref, tile_ea_ref, tile_eb_ref, nused_ref, incl_ref, *, tm):
    t = cls_ref.shape[1]
    ntp = tile_ea_ref.shape[1]
    lanes = 256
    cls = cls_ref[...]
    crow = lax.broadcasted_iota(jnp.int32, (CLASS_ROWS, t), 0)
    onehot = (crow == cls).astype(F32)
    k_i = lax.broadcasted_iota(jnp.int32, (lanes, lanes), 0)
    j_i = lax.broadcasted_iota(jnp.int32, (lanes, lanes), 1)
    upper = (k_i <= j_i).astype(BF16)
    carry = jnp.zeros((CLASS_ROWS, 1), F32)
    for c in range(t // lanes):
        sl = slice(c * lanes, (c + 1) * lanes)
        inc = jnp.dot(onehot[:, sl].astype(BF16), upper, preferred_element_type=F32) + carry
        incl_ref[:, sl] = inc
        carry = inc[:, lanes - 1:lanes]
    rank = jnp.sum(onehot * incl_ref[...], axis=0, keepdims=True) - 1.0
    ntile = jnp.floor((carry + (tm - 1)) * (1.0 / tm))
    srow = lax.broadcasted_iota(jnp.int32, (CLASS_ROWS, 1), 0)
    tstart = jnp.zeros((CLASS_ROWS, 1), F32)
    for c in range(N_CLASSES):
        tstart = tstart + jnp.where(srow > c, ntile[c:c + 1, :], 0.0)
    tend = tstart + ntile
    pos = jnp.sum(onehot * (tstart * tm), axis=0, keepdims=True) + rank
    pos_ref[...] = pos.astype(jnp.int32)
    nused = jnp.max(jnp.where(srow < N_CLASSES, tend, 0.0), axis=0, keepdims=True)
    last_cls = jnp.max(jnp.where((ntile > 0) & (srow < N_CLASSES), srow, 0), axis=0, keepdims=True)
    tile_i = lax.broadcasted_iota(jnp.int32, (CLASS_ROWS, ntp), 1).astype(F32)
    done = ((tile_i >= tend) & (lax.broadcasted_iota(jnp.int32, (CLASS_ROWS, ntp), 0) < N_CLASSES))
    tcls = jnp.minimum(jnp.sum(done.astype(jnp.int32), axis=0, keepdims=True), last_cls)
    ea = jnp.zeros((1, ntp), jnp.int32)
    eb = jnp.zeros((1, ntp), jnp.int32)
    for c in range(N_CLASSES):
        ea = jnp.where(tcls == c, _CLASS_EA[c], ea)
        eb = jnp.where(tcls == c, _CLASS_EB[c], eb)
    tile_ea_ref[...] = ea
    tile_eb_ref[...] = eb
    nused_ref[...] = jnp.broadcast_to(nused, nused_ref.shape).astype(jnp.int32)


def _positions(cls, tm, ntp):
    t = cls.shape[1]
    return pl.pallas_call(
        functools.partial(_positions_kernel, tm=tm),
        out_shape=[
            jax.ShapeDtypeStruct((1, t), jnp.int32),
            jax.ShapeDtypeStruct((1, ntp), jnp.int32),
            jax.ShapeDtypeStruct((1, ntp), jnp.int32),
            jax.ShapeDtypeStruct((1, 128), jnp.int32),
        ],
        scratch_shapes=[pltpu.VMEM((CLASS_ROWS, t), F32)],
        compiler_params=pltpu.CompilerParams(vmem_limit_bytes=V7X_VMEM_LIMIT),
        name="positions",
    )(cls)


def _invert_kernel(pos_ref, tok_ref):
    t = pos_ref.shape[0]
    r = tok_ref.shape[0]

    def fill(j, carry):
        for u in range(INVERT_UNROLL):
            tok_ref[j * INVERT_UNROLL + u] = 0
        return carry

    lax.fori_loop(0, r // INVERT_UNROLL, fill, 0)

    def put(j, carry):
        k0 = j * INVERT_UNROLL
        dst = [pos_ref[k0 + u] for u in range(INVERT_UNROLL)]
        for u in range(INVERT_UNROLL):
            tok_ref[dst[u]] = k0 + u
        return carry

    lax.fori_loop(0, t // INVERT_UNROLL, put, 0)


def _invert(pos, r):
    return pl.pallas_call(
        _invert_kernel,
        in_specs=[pl.BlockSpec(memory_space=pltpu.SMEM)],
        out_specs=pl.BlockSpec(memory_space=pltpu.SMEM),
        out_shape=jax.ShapeDtypeStruct((r,), jnp.int32),
        name="invert_positions",
    )(pos)


def _moe_kernel(tok_ref, ea_ref, eb_ref, nused_ref, x_hbm, rw_ref, wga_ref, wua_ref, wda_ref,
                wgb_ref, wub_ref, wdb_ref, g_ref, b_ref, out_ref, buf, sem):
    i = pl.program_id(0)
    tm, d = out_ref.shape
    nused = nused_ref[0]
    slot = lax.rem(i, 2)

    @pl.when((i == 0) & (nused > 0))
    def _():
        _start_gather(tok_ref, x_hbm, buf, sem, 0, 0, tm)

    @pl.when(i + 1 < nused)
    def _():
        _start_gather(tok_ref, x_hbm, buf, sem, 1 - slot, i + 1, tm)

    @pl.when(i < nused)
    def _():
        _wait_gather(x_hbm, buf, sem, slot, tm)
        x = buf[slot]
        xb = x.astype(BF16)
        scores = jax.nn.sigmoid(jnp.dot(xb, rw_ref[...], preferred_element_type=F32))
        lane = lax.broadcasted_iota(jnp.int32, scores.shape, 1)
        sa = jnp.sum(jnp.where(lane == ea_ref[i], scores, 0.0), axis=-1, keepdims=True)
        sb = jnp.sum(jnp.where(lane == eb_ref[i], scores, 0.0), axis=-1, keepdims=True)
        denom = sa + sb

        def expert(wg_ref, wu_ref, wd_ref):
            gt = jnp.dot(xb, wg_ref[...], preferred_element_type=F32)
            up = jnp.dot(xb, wu_ref[...], preferred_element_type=F32)
            hid = (jax.nn.silu(gt) * up).astype(BF16)
            return jnp.dot(hid, wd_ref[...], preferred_element_type=F32)

        f = (sa / denom) * expert(wga_ref, wua_ref, wda_ref)
        f = f + (sb / denom) * expert(wgb_ref, wub_ref, wdb_ref)
        out_ref[...] = _layer_norm(DEEPNORM_ALPHA * x + f, g_ref[...], b_ref[...])

    @pl.when(i >= nused)
    def _():
        out_ref[...] = jnp.zeros_like(out_ref)


def _moe(tok, tile_ea, tile_eb, nused, x1, rw_pad, wg, wu, wd, g, b, layer, tm):
    nt = tile_ea.shape[0]
    d = x1.shape[1]
    ff = wg.shape[-1]
    const = lambda shape: pl.BlockSpec(shape, lambda i, *_: (0,) * len(shape))
    up_a = pl.BlockSpec((None, None, d, ff), lambda i, tok, ea, eb, nu: (layer, ea[i], 0, 0))
    up_b = pl.BlockSpec((None, None, d, ff), lambda i, tok, ea, eb, nu: (layer, eb[i], 0, 0))
    dn_a = pl.BlockSpec((None, None, ff, d), lambda i, tok, ea, eb, nu: (layer, ea[i], 0, 0))
    dn_b = pl.BlockSpec((None, None, ff, d), lambda i, tok, ea, eb, nu: (layer, eb[i], 0, 0))
    return pl.pallas_call(
        _moe_kernel,
        grid_spec=pltpu.PrefetchScalarGridSpec(
            num_scalar_prefetch=4,
            grid=(nt,),
            in_specs=[
                pl.BlockSpec(memory_space=pl.ANY), const((d, 128)),
                up_a, up_a, dn_a, up_b, up_b, dn_b,
                const((1, d)), const((1, d)),
            ],
            out_specs=pl.BlockSpec((tm, d), lambda i, *_: (i, 0)),
            scratch_shapes=[pltpu.VMEM((2, tm, d), F32), pltpu.SemaphoreType.DMA((2,))],
        ),
        out_shape=jax.ShapeDtypeStruct((nt * tm, d), F32),
        compiler_params=pltpu.CompilerParams(
            dimension_semantics=("arbitrary",), vmem_limit_bytes=V7X_VMEM_LIMIT),
        name=f"moe_{layer}",
    )(tok, tile_ea, tile_eb, nused, x1, rw_pad, wg, wu, wd, wg, wu, wd, g, b)


def _unsort_kernel(pos_ref, xs_hbm, out_ref, buf, sem):
    i = pl.program_id(0)
    n = pl.num_programs(0)
    ts = out_ref.shape[0]
    slot = lax.rem(i, 2)

    @pl.when(i == 0)
    def _():
        _start_gather(pos_ref, xs_hbm, buf, sem, 0, 0, ts)

    @pl.when(i + 1 < n)
    def _():
        _start_gather(pos_ref, xs_hbm, buf, sem, 1 - slot, i + 1, ts)

    _wait_gather(xs_hbm, buf, sem, slot, ts)
    out_ref[...] = buf[slot]


def _unsort(pos, xs, ts):
    t = pos.shape[0]
    d = xs.shape[1]
    return pl.pallas_call(
        _unsort_kernel,
        grid_spec=pltpu.PrefetchScalarGridSpec(
            num_scalar_prefetch=1,
            grid=(t // ts,),
            in_specs=[pl.BlockSpec(memory_space=pl.ANY)],
            out_specs=pl.BlockSpec((ts, d), lambda i, p: (i, 0)),
            scratch_shapes=[pltpu.VMEM((2, ts, d), F32), pltpu.SemaphoreType.DMA((2,))],
        ),
        out_shape=jax.ShapeDtypeStruct((t, d), F32),
        compiler_params=pltpu.CompilerParams(dimension_semantics=("arbitrary",)),
        name="unsort",
    )(pos, xs)


def _route_and_experts(x1, cls, rw_pad, wg, wu, wd, g, b, layer, tm):
    t = x1.shape[0]
    nt = t // tm + N_CLASSES
    ntp = -(-nt // 128) * 128
    pos, tile_ea, tile_eb, nused = _positions(cls, tm, ntp)
    pos = pos.reshape(t)
    tok = _invert(pos, nt * tm)
    xs = _moe(tok, tile_ea.reshape(ntp)[:nt], tile_eb.reshape(ntp)[:nt], nused.reshape(128)[:1],
              x1, rw_pad, wg, wu, wd, g, b, layer, tm)
    return pos, xs


def _forward(x, a_w_in, a_conv_w, a_w_out, b_w_in, b_ln_g, b_ln_b, b_ws, b_bs, b_w_out,
             router_w, router_bias, moe_w_gate, moe_w_up, moe_w_down,
             ln_mix_g, ln_mix_b, ln_ffn_g, ln_ffn_b, *, ts_conv, ts_sgu, tm, ts_out):
    bn, s, d = x.shape
    rwt = router_w.T.astype(F32)
    rb = router_bias.astype(F32).reshape(N_EXPERTS, 1)
    rw_pad = jnp.pad(router_w, ((0, 0), (0, 128 - N_EXPERTS))).astype(BF16)
    wg = moe_w_gate.astype(BF16)
    wu = moe_w_up.astype(BF16)
    wd = moe_w_down.astype(BF16)
    row = lambda a: a.reshape(1, -1)

    x1, cls = _conv_mixer(x, a_w_in[0].astype(BF16), a_conv_w[0], a_w_out[0].astype(BF16),
                          row(ln_mix_g[0]), row(ln_mix_b[0]), rwt, rb, ts_conv)
    pos, xs = _route_and_experts(x1, cls, rw_pad, wg, wu, wd,
                                 row(ln_ffn_g[0]), row(ln_ffn_b[0]), 0, tm)
    x1, cls = _sgu_mixer(pos, xs, b_w_in[0].astype(BF16), row(b_ln_g[0]), row(b_ln_b[0]),
                         b_ws[0], b_bs[0].T, b_w_out[0].astype(BF16),
                         row(ln_mix_g[1]), row(ln_mix_b[1]), rwt, rb, ts_sgu)
    pos, xs = _route_and_experts(x1, cls, rw_pad, wg, wu, wd,
                                 row(ln_ffn_g[1]), row(ln_ffn_b[1]), 1, tm)
    return _unsort(pos, xs, ts_out).reshape(bn, s, d)


def kernel(x, a_w_in, a_conv_w, a_w_out, b_w_in, b_ln_g, b_ln_b, b_ws, b_bs, b_w_out, router_w, router_bias, moe_w_gate, moe_w_up, moe_w_down, ln_mix_g, ln_mix_b, ln_ffn_g, ln_ffn_b):
    return _forward(x, a_w_in, a_conv_w, a_w_out, b_w_in, b_ln_g, b_ln_b, b_ws, b_bs, b_w_out,
                    router_w, router_bias, moe_w_gate, moe_w_up, moe_w_down,
                    ln_mix_g, ln_mix_b, ln_ffn_g, ln_ffn_b,
                    ts_conv=512, ts_sgu=256, tm=256, ts_out=512)
```

```python
import functools

import jax
import jax.numpy as jnp
from jax import lax
from jax.experimental import pallas as pl
from jax.experimental.pallas import tpu as pltpu

F32 = jnp.float32
BF16 = jnp.bfloat16
HIGHEST = lax.Precision.HIGHEST

N_EXPERTS = 16
N_GROUPS = 4
EXPERTS_PER_GROUP = 4
CHUNK = 128
SGU_HEADS = 8
CONV_WIDTH = 3
DEPTH = 2
DEEPNORM_ALPHA = (2 * DEPTH) ** 0.25
LN_EPS = 1e-5

_PAIRS = ((0, 1), (0, 2), (0, 3), (1, 3), (2, 3), (2, 1))
N_PAIRS = len(_PAIRS)
N_CLASSES = N_GROUPS * N_PAIRS
CLASS_ROWS = 32
_CLASS_EA = tuple(g * EXPERTS_PER_GROUP + p[0] for g in range(N_GROUPS) for p in _PAIRS)
_CLASS_EB = tuple(g * EXPERTS_PER_GROUP + p[1] for g in range(N_GROUPS) for p in _PAIRS)

LANES = 128
SUBLANES = 8
V7X_VMEM_LIMIT = 56 * 1024 * 1024
ROW_UNROLL = 16
INVERT_UNROLL = 16


def _layer_norm(y, g, b):
    mu = jnp.mean(y, axis=-1, keepdims=True)
    d = y - mu
    var = jnp.mean(d * d, axis=-1, keepdims=True)
    return d * lax.rsqrt(var + LN_EPS) * g + b


def _gelu(x):
    return 0.5 * x * (1.0 + lax.erf(x * (2.0 ** -0.5)))


def _slab_load(ref, base, rows):
    return jnp.concatenate(
        [ref[pl.ds(base + s, rows, stride=SUBLANES), :] for s in range(SUBLANES)], axis=1)


def _slab_store(ref, base, val):
    rows = val.shape[0]
    for s in range(SUBLANES):
        ref[pl.ds(base + s, rows, stride=SUBLANES), :] = val[:, s * LANES:(s + 1) * LANES]


def _route_class(x1, rwt, rb):
    logits = lax.dot_general(rwt, x1, (((1,), (1,)), ((), ())), precision=HIGHEST,
                             preferred_element_type=F32)
    biased = jax.nn.sigmoid(logits) + rb
    v = [biased[e:e + 1, :] for e in range(N_EXPERTS)]
    best_score = None
    best_cls = None
    for g in range(N_GROUPS):
        vg = v[g * EXPERTS_PER_GROUP:(g + 1) * EXPERTS_PER_GROUP]
        sel = []
        for i in range(EXPERTS_PER_GROUP):
            rank = jnp.zeros_like(vg[i], dtype=jnp.int32)
            for j in range(EXPERTS_PER_GROUP):
                if j == i:
                    continue
                beats = (vg[j] >= vg[i]) if j < i else (vg[j] > vg[i])
                rank = rank + beats.astype(jnp.int32)
            sel.append(rank < 2)
        top_sum = None
        cls_g = jnp.zeros_like(vg[0], dtype=jnp.int32)
        for p, (a, b) in enumerate(_PAIRS):
            is_pair = sel[a] & sel[b]
            cls_g = jnp.where(is_pair, g * N_PAIRS + p, cls_g)
            pair_sum = vg[min(a, b)] + vg[max(a, b)]
            top_sum = jnp.where(is_pair, pair_sum, 0.0 if top_sum is None else top_sum)
        if best_score is None:
            best_score, best_cls = top_sum, cls_g
        else:
            better = top_sum > best_score
            best_score = jnp.where(better, top_sum, best_score)
            best_cls = jnp.where(better, cls_g, best_cls)
    return best_cls


def _conv_mixer_kernel(x_ref, w_in_ref, cw_ref, w_out_ref, g_ref, b_ref, rwt_ref, rb_ref,
                       x1_ref, cls_ref, carry_ref):
    ts, d = x_ref.shape

    @pl.when(pl.program_id(1) == 0)
    def _():
        carry_ref[...] = jnp.zeros_like(carry_ref)

    x = x_ref[...]
    xb = x.astype(BF16)
    cg = jnp.dot(xb, w_in_ref[:, d:2 * d], preferred_element_type=F32)
    h = jnp.dot(xb, w_in_ref[:, 2 * d:3 * d], preferred_element_type=F32)
    ch = cg * h
    prev2 = carry_ref[6:7, :]
    prev1 = carry_ref[7:8, :]
    rows = lax.broadcasted_iota(jnp.int32, (ts, 1), 0)
    ch1 = jnp.where(rows == 0, prev1, pltpu.roll(ch, 1, 0))
    ch2 = jnp.where(rows == 0, prev2, jnp.where(rows == 1, prev1, pltpu.roll(ch, 2, 0)))
    z = cw_ref[0:1, :] * ch2 + cw_ref[1:2, :] * ch1 + cw_ref[2:3, :] * ch
    carry_ref[...] = ch[ts - 8:ts, :]
    bg = jnp.dot(xb, w_in_ref[:, 0:d], preferred_element_type=F32)
    m = jnp.dot((bg * z).astype(BF16), w_out_ref[...], preferred_element_type=F32)
    x1 = _layer_norm(DEEPNORM_ALPHA * x + m, g_ref[...], b_ref[...])
    _slab_store(x1_ref, 0, x1)
    cls_ref[...] = _route_class(x1, rwt_ref[...], rb_ref[...])


def _conv_mixer(x, w_in, conv_w, w_out, ln_g, ln_b, rwt, rb, ts):
    bn, s, d = x.shape
    ns = s // ts
    const = lambda shape: pl.BlockSpec(shape, lambda b, i: (0,) * len(shape))
    return pl.pallas_call(
        _conv_mixer_kernel,
        grid=(bn, ns),
        in_specs=[
            pl.BlockSpec((None, ts, d), lambda b, i: (b, i, 0)),
            const((d, 3 * d)), const((CONV_WIDTH, d)), const((d, d)),
            const((1, d)), const((1, d)), const((N_EXPERTS, d)), const((N_EXPERTS, 1)),
        ],
        out_specs=[
            pl.BlockSpec((ts * SUBLANES, LANES), lambda b, i: (b * ns + i, 0)),
            pl.BlockSpec((1, ts), lambda b, i: (0, b * ns + i)),
        ],
        out_shape=[
            jax.ShapeDtypeStruct((bn * s * SUBLANES, LANES), F32),
            jax.ShapeDtypeStruct((1, bn * s), jnp.int32),
        ],
        scratch_shapes=[pltpu.VMEM((8, d), F32)],
        compiler_params=pltpu.CompilerParams(
            dimension_semantics=("arbitrary", "arbitrary"), vmem_limit_bytes=V7X_VMEM_LIMIT),
        name="conv_mixer",
    )(x, w_in, conv_w, w_out, ln_g, ln_b, rwt, rb)


def _sgu_mixer_kernel(xs_ref, w_in_ref, lng_ref, lnb_ref, ws_ref, bst_ref, w_out_ref,
                      g_ref, b_ref, rwt_ref, rb_ref, x1_ref, cls_ref, v_ref, gate_ref):
    ts, width = v_ref.shape
    hd = width // SGU_HEADS
    x = _slab_load(xs_ref, 0, ts)
    xb = x.astype(BF16)
    v = _gelu(jnp.dot(xb, w_in_ref[:, width:2 * width], preferred_element_type=F32))
    v_ref[...] = _layer_norm(v, lng_ref[...], lnb_ref[...]).astype(BF16)
    r_i = lax.broadcasted_iota(jnp.int32, (CHUNK, CHUNK), 0)
    c_i = lax.broadcasted_iota(jnp.int32, (CHUNK, CHUNK), 1)
    causal = r_i >= c_i
    for h in range(SGU_HEADS):
        cols = slice(h * hd, (h + 1) * hd)
        u_h = _gelu(jnp.dot(xb, w_in_ref[:, cols], preferred_element_type=F32))
        w_h = jnp.where(causal, ws_ref[h], 0.0).astype(BF16)
        bias = bst_ref[:, h:h + 1]
        for c in range(ts // CHUNK):
            rws = slice(c * CHUNK, (c + 1) * CHUNK)
            mixed = jnp.dot(w_h, v_ref[rws, cols], preferred_element_type=F32) + bias
            gate_ref[rws, cols] = (u_h[rws, :] * mixed).astype(BF16)
    m = jnp.dot(gate_ref[...], w_out_ref[...], preferred_element_type=F32)
    x1 = _layer_norm(DEEPNORM_ALPHA * x + m, g_ref[...], b_ref[...])
    _slab_store(x1_ref, 0, x1)
    cls_ref[...] = _route_class(x1, rwt_ref[...], rb_ref[...])


def _sgu_mixer(xs, w_in, ln_g, ln_b, ws, bst, w_out, g, b, rwt, rb, ts):
    t = xs.shape[0] // SUBLANES
    width, d = w_out.shape
    const = lambda shape: pl.BlockSpec(shape, lambda i: (0,) * len(shape))
    return pl.pallas_call(
        _sgu_mixer_kernel,
        grid=(t // ts,),
        in_specs=[
            pl.BlockSpec((ts * SUBLANES, LANES), lambda i: (i, 0)),
            const((d, 2 * width)), const((1, width)), const((1, width)),
            const((SGU_HEADS, CHUNK, CHUNK)), const((CHUNK, SGU_HEADS)), const((width, d)),
            const((1, d)), const((1, d)), const((N_EXPERTS, d)), const((N_EXPERTS, 1)),
        ],
        out_specs=[
            pl.BlockSpec((ts * SUBLANES, LANES), lambda i: (i, 0)),
            pl.BlockSpec((1, ts), lambda i: (0, i)),
        ],
        out_shape=[
            jax.ShapeDtypeStruct((t * SUBLANES, LANES), F32),
            jax.ShapeDtypeStruct((1, t), jnp.int32),
        ],
        scratch_shapes=[pltpu.VMEM((ts, width), BF16), pltpu.VMEM((ts, width), BF16)],
        compiler_params=pltpu.CompilerParams(
            dimension_semantics=("arbitrary",), vmem_limit_bytes=V7X_VMEM_LIMIT),
        name="sgu_mixer",
    )(xs, w_in, ln_g, ln_b, ws, bst, w_out, g, b, rwt, rb)


def _positions_kernel(cls_ref, pos_ref, tile_ea_ref, tile_eb_ref, nvalid_ref, nused_ref,
                      incl_ref, *, tm):
    t = cls_ref.shape[1]
    ntp = tile_ea_ref.shape[1]
    lanes = 256
    cls = cls_ref[...]
    crow = lax.broadcasted_iota(jnp.int32, (CLASS_ROWS, t), 0)
    onehot = (crow == cls).astype(F32)
    k_i = lax.broadcasted_iota(jnp.int32, (lanes, lanes), 0)
    j_i = lax.broadcasted_iota(jnp.int32, (lanes, lanes), 1)
    upper = (k_i <= j_i).astype(BF16)
    count = jnp.zeros((CLASS_ROWS, 1), F32)
    for c in range(t // lanes):
        sl = slice(c * lanes, (c + 1) * lanes)
        inc = jnp.dot(onehot[:, sl].astype(BF16), upper, preferred_element_type=F32) + count
        incl_ref[:, sl] = inc
        count = inc[:, lanes - 1:lanes]
    rank = jnp.sum(onehot * incl_ref[...], axis=0, keepdims=True) - 1.0
    ntile = jnp.floor((count + (tm - 1)) * (1.0 / tm))
    srow = lax.broadcasted_iota(jnp.int32, (CLASS_ROWS, 1), 0)
    tstart = jnp.zeros((CLASS_ROWS, 1), F32)
    for c in range(N_CLASSES):
        tstart = tstart + jnp.where(srow > c, ntile[c:c + 1, :], 0.0)
    tend = tstart + ntile
    pos = jnp.sum(onehot * (tstart * tm), axis=0, keepdims=True) + rank
    pos_ref[...] = pos.astype(jnp.int32)
    nused = jnp.max(jnp.where(srow < N_CLASSES, tend, 0.0), axis=0, keepdims=True)
    last_cls = jnp.max(jnp.where((ntile > 0) & (srow < N_CLASSES), srow, 0), axis=0, keepdims=True)
    tile_i = lax.broadcasted_iota(jnp.int32, (CLASS_ROWS, ntp), 1).astype(F32)
    is_cls = lax.broadcasted_iota(jnp.int32, (CLASS_ROWS, ntp), 0) < N_CLASSES
    done = (tile_i >= tend) & is_cls
    tcls = jnp.minimum(jnp.sum(done.astype(jnp.int32), axis=0, keepdims=True), last_cls)
    ea = jnp.zeros((1, ntp), jnp.int32)
    eb = jnp.zeros((1, ntp), jnp.int32)
    for c in range(N_CLASSES):
        ea = jnp.where(tcls == c, _CLASS_EA[c], ea)
        eb = jnp.where(tcls == c, _CLASS_EB[c], eb)
    tile_ea_ref[...] = ea
    tile_eb_ref[...] = eb
    inside = (tile_i >= tstart) & (tile_i < tend) & is_cls
    left = jnp.minimum(count - (tile_i - tstart) * tm, float(tm))
    nvalid_ref[...] = jnp.sum(jnp.where(inside, left, 0.0), axis=0, keepdims=True).astype(jnp.int32)
    nused_ref[...] = jnp.broadcast_to(nused, nused_ref.shape).astype(jnp.int32)


def _positions(cls, tm, ntp):
    t = cls.shape[1]
    return pl.pallas_call(
        functools.partial(_positions_kernel, tm=tm),
        out_shape=[
            jax.ShapeDtypeStruct((1, t), jnp.int32),
            jax.ShapeDtypeStruct((1, ntp), jnp.int32),
            jax.ShapeDtypeStruct((1, ntp), jnp.int32),
            jax.ShapeDtypeStruct((1, ntp), jnp.int32),
            jax.ShapeDtypeStruct((1, 128), jnp.int32),
        ],
        scratch_shapes=[pltpu.VMEM((CLASS_ROWS, t), F32)],
        compiler_params=pltpu.CompilerParams(vmem_limit_bytes=V7X_VMEM_LIMIT),
        name="positions",
    )(cls)


def _invert_kernel(pos_ref, tok_ref):
    t = pos_ref.shape[0]
    r = tok_ref.shape[0]

    def fill(j, carry):
        for u in range(INVERT_UNROLL):
            tok_ref[j * INVERT_UNROLL + u] = 0
        return carry

    lax.fori_loop(0, r // INVERT_UNROLL, fill, 0)

    def put(j, carry):
        k0 = j * INVERT_UNROLL
        dst = [pos_ref[k0 + u] for u in range(INVERT_UNROLL)]
        for u in range(INVERT_UNROLL):
            tok_ref[dst[u]] = k0 + u
        return carry

    lax.fori_loop(0, t // INVERT_UNROLL, put, 0)


def _invert(pos, r):
    return pl.pallas_call(
        _invert_kernel,
        in_specs=[pl.BlockSpec(memory_space=pltpu.SMEM)],
        out_specs=pl.BlockSpec(memory_space=pltpu.SMEM),
        out_shape=jax.ShapeDtypeStruct((r,), jnp.int32),
        name="invert_positions",
    )(pos)


def _for_rows(n, per_row, per_group=None):
    ngroups = n // ROW_UNROLL

    def group(j, carry):
        if per_group is not None:
            per_group()
        else:
            for u in range(ROW_UNROLL):
                per_row(j * ROW_UNROLL + u)
        return carry

    lax.fori_loop(0, ngroups, group, 0)

    def single(r, carry):
        per_row(r)
        return carry

    lax.fori_loop(ngroups * ROW_UNROLL, n, single, 0)


def _moe_kernel(tok_ref, ea_ref, eb_ref, nvalid_ref, nused_ref, x_hbm, rw_ref, wga_ref, wua_ref,
                wda_ref, wgb_ref, wub_ref, wdb_ref, g_ref, b_ref, out_hbm, gbuf, gsem, obuf, osem,
                *, tm, out_slab):
    i = pl.program_id(0)
    nt = pl.num_programs(0)
    nused = nused_ref[0]
    slot = lax.rem(i, 2)
    slab_rows = tm * SUBLANES

    def gather_copy(tile, sl, r):
        dst = pl.multiple_of((sl * tm + r) * SUBLANES, SUBLANES)
        return pltpu.make_async_copy(x_hbm.at[tok_ref[tile * tm + r]],
                                     gbuf.at[pl.ds(dst, SUBLANES), :], gsem.at[sl])

    def start_gather(tile, sl):
        _for_rows(nvalid_ref[tile], lambda r: gather_copy(tile, sl, r).start())

    def wait_gather(tile, sl):
        group = pltpu.make_async_copy(gbuf.at[pl.ds(0, ROW_UNROLL * SUBLANES), :],
                                      gbuf.at[pl.ds(0, ROW_UNROLL * SUBLANES), :], gsem.at[sl])
        _for_rows(nvalid_ref[tile], lambda r: gather_copy(tile, sl, r).wait(), group.wait)

    def scatter_copy(tile, sl, r):
        tok = tok_ref[tile * tm + r]
        if out_slab:
            src = pl.multiple_of((sl * tm + r) * SUBLANES, SUBLANES)
            return pltpu.make_async_copy(obuf.at[pl.ds(src, SUBLANES), :], out_hbm.at[tok],
                                         osem.at[sl])
        return pltpu.make_async_copy(obuf.at[sl, pl.ds(r, 1), :], out_hbm.at[pl.ds(tok, 1), :],
                                     osem.at[sl])

    def start_scatter(tile, sl):
        _for_rows(nvalid_ref[tile], lambda r: scatter_copy(tile, sl, r).start())

    def wait_scatter(tile, sl):
        if out_slab:
            part = obuf.at[pl.ds(0, ROW_UNROLL * SUBLANES), :]
        else:
            part = obuf.at[sl, pl.ds(0, ROW_UNROLL), :]
        group = pltpu.make_async_copy(part, part, osem.at[sl])
        _for_rows(nvalid_ref[tile], lambda r: scatter_copy(tile, sl, r).wait(), group.wait)

    @pl.when(i == 0)
    def _():
        gbuf[...] = jnp.zeros_like(gbuf)

    @pl.when((i == 0) & (nused > 0))
    def _():
        start_gather(0, 0)

    @pl.when(i + 1 < nused)
    def _():
        start_gather(i + 1, 1 - slot)

    @pl.when((i >= 2) & (i - 2 < nused))
    def _():
        wait_scatter(i - 2, slot)

    @pl.when(i < nused)
    def _():
        wait_gather(i, slot)
        x = _slab_load(gbuf, pl.multiple_of(slot * slab_rows, SUBLANES), tm)
        xb = x.astype(BF16)
        scores = jax.nn.sigmoid(jnp.dot(xb, rw_ref[...], preferred_element_type=F32))
        lane = lax.broadcasted_iota(jnp.int32, scores.shape, 1)
        sa = jnp.sum(jnp.where(lane == ea_ref[i], scores, 0.0), axis=-1, keepdims=True)
        sb = jnp.sum(jnp.where(lane == eb_ref[i], scores, 0.0), axis=-1, keepdims=True)
        denom = sa + sb

        def expert(wg_ref, wu_ref, wd_ref):
            gt = jnp.dot(xb, wg_ref[...], preferred_element_type=F32)
            up = jnp.dot(xb, wu_ref[...], preferred_element_type=F32)
            hid = (jax.nn.silu(gt) * up).astype(BF16)
            return jnp.dot(hid, wd_ref[...], preferred_element_type=F32)

        f = (sa / denom) * expert(wga_ref, wua_ref, wda_ref)
        f = f + (sb / denom) * expert(wgb_ref, wub_ref, wdb_ref)
        y = _layer_norm(DEEPNORM_ALPHA * x + f, g_ref[...], b_ref[...])
        if out_slab:
            _slab_store(obuf, pl.multiple_of(slot * slab_rows, SUBLANES), y)
        else:
            obuf[slot] = y
        start_scatter(i, slot)

    @pl.when(i == nt - 1)
    def _():
        for back in (1, 0):
            tile = i - back

            @pl.when((tile >= 0) & (tile < nused))
            def _():
                wait_scatter(tile, lax.rem(tile, 2))


def _moe(tok, tile_ea, tile_eb, nvalid, nused, x1, rw_pad, wg, wu, wd, g, b, layer, tm, out_slab):
    nt = tile_ea.shape[0]
    t = x1.shape[0]
    d = SUBLANES * LANES
    ff = wg.shape[-1]
    const = lambda shape: pl.BlockSpec(shape, lambda i, *_: (0,) * len(shape))
    up_a = pl.BlockSpec((None, None, d, ff), lambda i, tok, ea, *_: (layer, ea[i], 0, 0))
    up_b = pl.BlockSpec((None, None, d, ff), lambda i, tok, ea, eb, *_: (layer, eb[i], 0, 0))
    dn_a = pl.BlockSpec((None, None, ff, d), lambda i, tok, ea, *_: (layer, ea[i], 0, 0))
    dn_b = pl.BlockSpec((None, None, ff, d), lambda i, tok, ea, eb, *_: (layer, eb[i], 0, 0))
    if out_slab:
        out_shape = jax.ShapeDtypeStruct((t, SUBLANES, LANES), F32)
        obuf = pltpu.VMEM((2 * tm * SUBLANES, LANES), F32)
    else:
        out_shape = jax.ShapeDtypeStruct((t, d), F32)
        obuf = pltpu.VMEM((2, tm, d), F32)
    return pl.pallas_call(
        functools.partial(_moe_kernel, tm=tm, out_slab=out_slab),
        grid_spec=pltpu.PrefetchScalarGridSpec(
            num_scalar_prefetch=5,
            grid=(nt,),
            in_specs=[
                pl.BlockSpec(memory_space=pl.ANY), const((d, 128)),
                up_a, up_a, dn_a, up_b, up_b, dn_b,
                const((1, d)), const((1, d)),
            ],
            out_specs=pl.BlockSpec(memory_space=pl.ANY),
            scratch_shapes=[
                pltpu.VMEM((2 * tm * SUBLANES, LANES), F32), pltpu.SemaphoreType.DMA((2,)),
                obuf, pltpu.SemaphoreType.DMA((2,)),
            ],
        ),
        out_shape=out_shape,
        compiler_params=pltpu.CompilerParams(
            dimension_semantics=("arbitrary",), vmem_limit_bytes=V7X_VMEM_LIMIT),
        name=f"moe_{layer}",
    )(tok, tile_ea, tile_eb, nvalid, nused, x1, rw_pad, wg, wu, wd, wg, wu, wd, g, b)


def _route_and_experts(x1_slab, cls, rw_pad, wg, wu, wd, g, b, layer, tm, out_slab):
    t = cls.shape[1]
    nt = t // tm + N_CLASSES
    ntp = -(-nt // 128) * 128
    pos, tile_ea, tile_eb, nvalid, nused = _positions(cls, tm, ntp)
    tok = _invert(pos.reshape(t), nt * tm)
    per_tile = lambda a: a.reshape(ntp)[:nt]
    return _moe(tok, per_tile(tile_ea), per_tile(tile_eb), per_tile(nvalid), nused.reshape(128)[:1],
                x1_slab.reshape(t, SUBLANES, LANES), rw_pad, wg, wu, wd, g, b, layer, tm, out_slab)


def _forward(x, a_w_in, a_conv_w, a_w_out, b_w_in, b_ln_g, b_ln_b, b_ws, b_bs, b_w_out,
             router_w, router_bias, moe_w_gate, moe_w_up, moe_w_down,
             ln_mix_g, ln_mix_b, ln_ffn_g, ln_ffn_b, *, ts_conv, ts_sgu, tm):
    bn, s, d = x.shape
    assert d == SUBLANES * LANES
    t = bn * s
    rwt = router_w.T.astype(F32)
    rb = router_bias.astype(F32).reshape(N_EXPERTS, 1)
    rw_pad = jnp.pad(router_w, ((0, 0), (0, 128 - N_EXPERTS))).astype(BF16)
    wg = moe_w_gate.astype(BF16)
    wu = moe_w_up.astype(BF16)
    wd = moe_w_down.astype(BF16)
    row = lambda a: a.reshape(1, -1)

    x1, cls = _conv_mixer(x, a_w_in[0].astype(BF16), a_conv_w[0], a_w_out[0].astype(BF16),
                          row(ln_mix_g[0]), row(ln_mix_b[0]), rwt, rb, ts_conv)
    x2 = _route_and_experts(x1, cls, rw_pad, wg, wu, wd,
                            row(ln_ffn_g[0]), row(ln_ffn_b[0]), 0, tm, True)
    x3, cls = _sgu_mixer(x2.reshape(t * SUBLANES, LANES), b_w_in[0].astype(BF16),
                         row(b_ln_g[0]), row(b_ln_b[0]), b_ws[0], b_bs[0].T,
                         b_w_out[0].astype(BF16), row(ln_mix_g[1]), row(ln_mix_b[1]), rwt, rb,
                         ts_sgu)
    x4 = _route_and_experts(x3, cls, rw_pad, wg, wu, wd,
                            row(ln_ffn_g[1]), row(ln_ffn_b[1]), 1, tm, False)
    return x4.reshape(bn, s, d)


def kernel(x, a_w_in, a_conv_w, a_w_out, b_w_in, b_ln_g, b_ln_b, b_ws, b_bs, b_w_out, router_w, router_bias, moe_w_gate, moe_w_up, moe_w_down, ln_mix_g, ln_mix_b, ln_ffn_g, ln_ffn_b):
    return _forward(x, a_w_in, a_conv_w, a_w_out, b_w_in, b_ln_g, b_ln_b, b_ws, b_bs, b_w_out,
                    router_w, router_bias, moe_w_gate, moe_w_up, moe_w_down,
                    ln_mix_g, ln_mix_b, ln_ffn_g, ln_ffn_b,
                    ts_conv=512, ts_sgu=256, tm=256)
```

```python
import functools

import jax
import jax.numpy as jnp
from jax import lax
from jax.experimental import pallas as pl
from jax.experimental.pallas import tpu as pltpu

F32 = jnp.float32
BF16 = jnp.bfloat16
HIGHEST = lax.Precision.HIGHEST

N_EXPERTS = 16
N_GROUPS = 4
EXPERTS_PER_GROUP = 4
CHUNK = 128
SGU_HEADS = 8
CONV_WIDTH = 3
DEPTH = 2
DEEPNORM_ALPHA = (2 * DEPTH) ** 0.25
LN_EPS = 1e-5

_PAIRS = ((0, 1), (0, 2), (0, 3), (1, 3), (2, 3), (2, 1))
N_PAIRS = len(_PAIRS)
N_CLASSES = N_GROUPS * N_PAIRS
CLASS_ROWS = 32
_CLASS_EA = tuple(g * EXPERTS_PER_GROUP + p[0] for g in range(N_GROUPS) for p in _PAIRS)
_CLASS_EB = tuple(g * EXPERTS_PER_GROUP + p[1] for g in range(N_GROUPS) for p in _PAIRS)

LANES = 128
SUBLANES = 8
V7X_VMEM_LIMIT = 56 * 1024 * 1024
ROW_UNROLL = 16
INVERT_UNROLL = 16


def _layer_norm(y, g, b):
    mu = jnp.mean(y, axis=-1, keepdims=True)
    d = y - mu
    var = jnp.mean(d * d, axis=-1, keepdims=True)
    return d * lax.rsqrt(var + LN_EPS) * g + b


def _gelu(x):
    return 0.5 * x * (1.0 + lax.erf(x * (2.0 ** -0.5)))


def _slab_load(ref, base, rows):
    return jnp.concatenate(
        [ref[pl.ds(base + s, rows, stride=SUBLANES), :] for s in range(SUBLANES)], axis=1)


def _slab_store(ref, base, val):
    rows = val.shape[0]
    for s in range(SUBLANES):
        ref[pl.ds(base + s, rows, stride=SUBLANES), :] = val[:, s * LANES:(s + 1) * LANES]


def _split_bf16(a):
    hi = a.astype(BF16)
    return hi, (a - hi.astype(F32)).astype(BF16)


def _route_class(x1, rw_split, rb):
    x_hi, x_lo = _split_bf16(x1)
    nt_dims = (((1,), (1,)), ((), ()))
    by_hi = lax.dot_general(rw_split, x_hi, nt_dims, preferred_element_type=F32)
    by_lo = lax.dot_general(rw_split[:N_EXPERTS, :], x_lo, nt_dims, preferred_element_type=F32)
    logits = by_hi[:N_EXPERTS, :] + (by_hi[N_EXPERTS:, :] + by_lo)
    biased = jax.nn.sigmoid(logits) + rb
    v = [biased[e:e + 1, :] for e in range(N_EXPERTS)]
    best_score = None
    best_cls = None
    for g in range(N_GROUPS):
        vg = v[g * EXPERTS_PER_GROUP:(g + 1) * EXPERTS_PER_GROUP]
        sel = []
        for i in range(EXPERTS_PER_GROUP):
            rank = jnp.zeros_like(vg[i], dtype=jnp.int32)
            for j in range(EXPERTS_PER_GROUP):
                if j == i:
                    continue
                beats = (vg[j] >= vg[i]) if j < i else (vg[j] > vg[i])
                rank = rank + beats.astype(jnp.int32)
            sel.append(rank < 2)
        top_sum = None
        cls_g = jnp.zeros_like(vg[0], dtype=jnp.int32)
        for p, (a, b) in enumerate(_PAIRS):
            is_pair = sel[a] & sel[b]
            cls_g = jnp.where(is_pair, g * N_PAIRS + p, cls_g)
            pair_sum = vg[min(a, b)] + vg[max(a, b)]
            top_sum = jnp.where(is_pair, pair_sum, 0.0 if top_sum is None else top_sum)
        if best_score is None:
            best_score, best_cls = top_sum, cls_g
        else:
            better = top_sum > best_score
            best_score = jnp.where(better, top_sum, best_score)
            best_cls = jnp.where(better, cls_g, best_cls)
    return best_cls


def _conv_mixer_kernel(x_ref, w_in_ref, cw_ref, w_out_ref, g_ref, b_ref, rwt_ref, rb_ref,
                       x1_ref, cls_ref, carry_ref):
    ts, d = x_ref.shape

    @pl.when(pl.program_id(1) == 0)
    def _():
        carry_ref[...] = jnp.zeros_like(carry_ref)

    x = x_ref[...]
    xb = x.astype(BF16)
    cg = jnp.dot(xb, w_in_ref[:, d:2 * d], preferred_element_type=F32)
    h = jnp.dot(xb, w_in_ref[:, 2 * d:3 * d], preferred_element_type=F32)
    ch = cg * h
    prev2 = carry_ref[6:7, :]
    prev1 = carry_ref[7:8, :]
    rows = lax.broadcasted_iota(jnp.int32, (ts, 1), 0)
    ch1 = jnp.where(rows == 0, prev1, pltpu.roll(ch, 1, 0))
    ch2 = jnp.where(rows == 0, prev2, jnp.where(rows == 1, prev1, pltpu.roll(ch, 2, 0)))
    z = cw_ref[0:1, :] * ch2 + cw_ref[1:2, :] * ch1 + cw_ref[2:3, :] * ch
    carry_ref[...] = ch[ts - 8:ts, :]
    bg = jnp.dot(xb, w_in_ref[:, 0:d], preferred_element_type=F32)
    m = jnp.dot((bg * z).astype(BF16), w_out_ref[...], preferred_element_type=F32)
    x1 = _layer_norm(DEEPNORM_ALPHA * x + m, g_ref[...], b_ref[...])
    _slab_store(x1_ref, 0, x1)
    cls_ref[...] = _route_class(x1, rwt_ref[...], rb_ref[...])


def _conv_mixer(x, w_in, conv_w, w_out, ln_g, ln_b, rwt, rb, ts):
    bn, s, d = x.shape
    ns = s // ts
    const = lambda shape: pl.BlockSpec(shape, lambda b, i: (0,) * len(shape))
    return pl.pallas_call(
        _conv_mixer_kernel,
        grid=(bn, ns),
        in_specs=[
            pl.BlockSpec((None, ts, d), lambda b, i: (b, i, 0)),
            const((d, 3 * d)), const((CONV_WIDTH, d)), const((d, d)),
            const((1, d)), const((1, d)), const((2 * N_EXPERTS, d)), const((N_EXPERTS, 1)),
        ],
        out_specs=[
            pl.BlockSpec((ts * SUBLANES, LANES), lambda b, i: (b * ns + i, 0)),
            pl.BlockSpec((1, ts), lambda b, i: (0, b * ns + i)),
        ],
        out_shape=[
            jax.ShapeDtypeStruct((bn * s * SUBLANES, LANES), F32),
            jax.ShapeDtypeStruct((1, bn * s), jnp.int32),
        ],
        scratch_shapes=[pltpu.VMEM((8, d), F32)],
        compiler_params=pltpu.CompilerParams(
            dimension_semantics=("arbitrary", "arbitrary"), vmem_limit_bytes=V7X_VMEM_LIMIT),
        name="conv_mixer",
    )(x, w_in, conv_w, w_out, ln_g, ln_b, rwt, rb)


def _sgu_mixer_kernel(xs_ref, w_in_ref, lng_ref, lnb_ref, ws_ref, bst_ref, w_out_ref,
                      g_ref, b_ref, rwt_ref, rb_ref, x1_ref, cls_ref, v_ref, gate_ref):
    ts, width = v_ref.shape
    hd = width // SGU_HEADS
    x = _slab_load(xs_ref, 0, ts)
    xb = x.astype(BF16)
    v = _gelu(jnp.dot(xb, w_in_ref[:, width:2 * width], preferred_element_type=F32))
    v_ref[...] = _layer_norm(v, lng_ref[...], lnb_ref[...]).astype(BF16)
    r_i = lax.broadcasted_iota(jnp.int32, (CHUNK, CHUNK), 0)
    c_i = lax.broadcasted_iota(jnp.int32, (CHUNK, CHUNK), 1)
    causal = r_i >= c_i
    for h in range(SGU_HEADS):
        cols = slice(h * hd, (h + 1) * hd)
        u_h = _gelu(jnp.dot(xb, w_in_ref[:, cols], preferred_element_type=F32))
        w_h = jnp.where(causal, ws_ref[h], 0.0).astype(BF16)
        bias = bst_ref[:, h:h + 1]
        for c in range(ts // CHUNK):
            rws = slice(c * CHUNK, (c + 1) * CHUNK)
            mixed = jnp.dot(w_h, v_ref[rws, cols], preferred_element_type=F32) + bias
            gate_ref[rws, cols] = (u_h[rws, :] * mixed).astype(BF16)
    m = jnp.dot(gate_ref[...], w_out_ref[...], preferred_element_type=F32)
    x1 = _layer_norm(DEEPNORM_ALPHA * x + m, g_ref[...], b_ref[...])
    _slab_store(x1_ref, 0, x1)
    cls_ref[...] = _route_class(x1, rwt_ref[...], rb_ref[...])


def _sgu_mixer(xs, w_in, ln_g, ln_b, ws, bst, w_out, g, b, rwt, rb, ts):
    t = xs.shape[0] // SUBLANES
    width, d = w_out.shape
    const = lambda shape: pl.BlockSpec(shape, lambda i: (0,) * len(shape))
    return pl.pallas_call(
        _sgu_mixer_kernel,
        grid=(t // ts,),
        in_specs=[
            pl.BlockSpec((ts * SUBLANES, LANES), lambda i: (i, 0)),
            const((d, 2 * width)), const((1, width)), const((1, width)),
            const((SGU_HEADS, CHUNK, CHUNK)), const((CHUNK, SGU_HEADS)), const((width, d)),
            const((1, d)), const((1, d)), const((2 * N_EXPERTS, d)), const((N_EXPERTS, 1)),
        ],
        out_specs=[
            pl.BlockSpec((ts * SUBLANES, LANES), lambda i: (i, 0)),
            pl.BlockSpec((1, ts), lambda i: (0, i)),
        ],
        out_shape=[
            jax.ShapeDtypeStruct((t * SUBLANES, LANES), F32),
            jax.ShapeDtypeStruct((1, t), jnp.int32),
        ],
        scratch_shapes=[pltpu.VMEM((ts, width), BF16), pltpu.VMEM((ts, width), BF16)],
        compiler_params=pltpu.CompilerParams(
            dimension_semantics=("arbitrary",), vmem_limit_bytes=V7X_VMEM_LIMIT),
        name="sgu_mixer",
    )(xs, w_in, ln_g, ln_b, ws, bst, w_out, g, b, rwt, rb)


def _positions_kernel(cls_ref, pos_ref, tile_ea_ref, tile_eb_ref, nvalid_ref, nused_ref,
                      incl_ref, *, tm):
    t = cls_ref.shape[1]
    ntp = tile_ea_ref.shape[1]
    lanes = 256
    cls = cls_ref[...]
    crow = lax.broadcasted_iota(jnp.int32, (CLASS_ROWS, t), 0)
    onehot = (crow == cls).astype(F32)
    k_i = lax.broadcasted_iota(jnp.int32, (lanes, lanes), 0)
    j_i = lax.broadcasted_iota(jnp.int32, (lanes, lanes), 1)
    upper = (k_i <= j_i).astype(BF16)
    count = jnp.zeros((CLASS_ROWS, 1), F32)
    for c in range(t // lanes):
        sl = slice(c * lanes, (c + 1) * lanes)
        inc = jnp.dot(onehot[:, sl].astype(BF16), upper, preferred_element_type=F32) + count
        incl_ref[:, sl] = inc
        count = inc[:, lanes - 1:lanes]
    rank = jnp.sum(onehot * incl_ref[...], axis=0, keepdims=True) - 1.0
    ntile = jnp.floor((count + (tm - 1)) * (1.0 / tm))
    srow = lax.broadcasted_iota(jnp.int32, (CLASS_ROWS, 1), 0)
    tstart = jnp.zeros((CLASS_ROWS, 1), F32)
    for c in range(N_CLASSES):
        tstart = tstart + jnp.where(srow > c, ntile[c:c + 1, :], 0.0)
    tend = tstart + ntile
    pos = jnp.sum(onehot * (tstart * tm), axis=0, keepdims=True) + rank
    pos_ref[...] = pos.astype(jnp.int32)
    nused = jnp.max(jnp.where(srow < N_CLASSES, tend, 0.0), axis=0, keepdims=True)
    last_cls = jnp.max(jnp.where((ntile > 0) & (srow < N_CLASSES), srow, 0), axis=0, keepdims=True)
    tile_i = lax.broadcasted_iota(jnp.int32, (CLASS_ROWS, ntp), 1).astype(F32)
    is_cls = lax.broadcasted_iota(jnp.int32, (CLASS_ROWS, ntp), 0) < N_CLASSES
    done = (tile_i >= tend) & is_cls
    tcls = jnp.minimum(jnp.sum(done.astype(jnp.int32), axis=0, keepdims=True), last_cls)
    ea = jnp.zeros((1, ntp), jnp.int32)
    eb = jnp.zeros((1, ntp), jnp.int32)
    for c in range(N_CLASSES):
        ea = jnp.where(tcls == c, _CLASS_EA[c], ea)
        eb = jnp.where(tcls == c, _CLASS_EB[c], eb)
    tile_ea_ref[...] = ea
    tile_eb_ref[...] = eb
    inside = (tile_i >= tstart) & (tile_i < tend) & is_cls
    left = jnp.minimum(count - (tile_i - tstart) * tm, float(tm))
    nvalid_ref[...] = jnp.sum(jnp.where(inside, left, 0.0), axis=0, keepdims=True).astype(jnp.int32)
    nused_ref[...] = jnp.broadcast_to(nused, nused_ref.shape).astype(jnp.int32)


def _positions(cls, tm, ntp):
    t = cls.shape[1]
    return pl.pallas_call(
        functools.partial(_positions_kernel, tm=tm),
        out_shape=[
            jax.ShapeDtypeStruct((1, t), jnp.int32),
            jax.ShapeDtypeStruct((1, ntp), jnp.int32),
            jax.ShapeDtypeStruct((1, ntp), jnp.int32),
            jax.ShapeDtypeStruct((1, ntp), jnp.int32),
            jax.ShapeDtypeStruct((1, 128), jnp.int32),
        ],
        scratch_shapes=[pltpu.VMEM((CLASS_ROWS, t), F32)],
        compiler_params=pltpu.CompilerParams(vmem_limit_bytes=V7X_VMEM_LIMIT),
        name="positions",
    )(cls)


def _invert_kernel(pos_ref, tok_ref):
    t = pos_ref.shape[0]
    r = tok_ref.shape[0]

    def fill(j, carry):
        for u in range(INVERT_UNROLL):
            tok_ref[j * INVERT_UNROLL + u] = 0
        return carry

    lax.fori_loop(0, r // INVERT_UNROLL, fill, 0)

    def put(j, carry):
        k0 = j * INVERT_UNROLL
        dst = [pos_ref[k0 + u] for u in range(INVERT_UNROLL)]
        for u in range(INVERT_UNROLL):
            tok_ref[dst[u]] = k0 + u
        return carry

    lax.fori_loop(0, t // INVERT_UNROLL, put, 0)


def _invert(pos, r):
    return pl.pallas_call(
        _invert_kernel,
        in_specs=[pl.BlockSpec(memory_space=pltpu.SMEM)],
        out_specs=pl.BlockSpec(memory_space=pltpu.SMEM),
        out_shape=jax.ShapeDtypeStruct((r,), jnp.int32),
        name="invert_positions",
    )(pos)


def _for_rows(n, per_row, per_group=None):
    ngroups = n // ROW_UNROLL

    def group(j, carry):
        if per_group is not None:
            per_group()
        else:
            for u in range(ROW_UNROLL):
                per_row(j * ROW_UNROLL + u)
        return carry

    lax.fori_loop(0, ngroups, group, 0)

    def single(r, carry):
        per_row(r)
        return carry

    lax.fori_loop(ngroups * ROW_UNROLL, n, single, 0)


def _moe_kernel(tok_ref, ea_ref, eb_ref, nvalid_ref, nused_ref, x_hbm, rw_ref, wga_ref, wua_ref,
                wda_ref, wgb_ref, wub_ref, wdb_ref, g_ref, b_ref, out_hbm, gbuf, gsem, obuf, osem,
                *, tm, out_slab):
    i = pl.program_id(0)
    nt = pl.num_programs(0)
    nused = nused_ref[0]
    slot = lax.rem(i, 2)
    slab_rows = tm * SUBLANES

    def gather_copy(tile, sl, r):
        dst = pl.multiple_of((sl * tm + r) * SUBLANES, SUBLANES)
        return pltpu.make_async_copy(x_hbm.at[tok_ref[tile * tm + r]],
                                     gbuf.at[pl.ds(dst, SUBLANES), :], gsem.at[sl])

    def start_gather(tile, sl):
        _for_rows(nvalid_ref[tile], lambda r: gather_copy(tile, sl, r).start())

    def wait_gather(tile, sl):
        group = pltpu.make_async_copy(gbuf.at[pl.ds(0, ROW_UNROLL * SUBLANES), :],
                                      gbuf.at[pl.ds(0, ROW_UNROLL * SUBLANES), :], gsem.at[sl])
        _for_rows(nvalid_ref[tile], lambda r: gather_copy(tile, sl, r).wait(), group.wait)

    def scatter_copy(tile, sl, r):
        tok = tok_ref[tile * tm + r]
        if out_slab:
            src = pl.multiple_of((sl * tm + r) * SUBLANES, SUBLANES)
            return pltpu.make_async_copy(obuf.at[pl.ds(src, SUBLANES), :], out_hbm.at[tok],
                                         osem.at[sl])
        return pltpu.make_async_copy(obuf.at[sl, pl.ds(r, 1), :], out_hbm.at[pl.ds(tok, 1), :],
                                     osem.at[sl])

    def start_scatter(tile, sl):
        _for_rows(nvalid_ref[tile], lambda r: scatter_copy(tile, sl, r).start())

    def wait_scatter(tile, sl):
        if out_slab:
            part = obuf.at[pl.ds(0, ROW_UNROLL * SUBLANES), :]
        else:
            part = obuf.at[sl, pl.ds(0, ROW_UNROLL), :]
        group = pltpu.make_async_copy(part, part, osem.at[sl])
        _for_rows(nvalid_ref[tile], lambda r: scatter_copy(tile, sl, r).wait(), group.wait)

    @pl.when(i == 0)
    def _():
        gbuf[...] = jnp.zeros_like(gbuf)

    @pl.when((i == 0) & (nused > 0))
    def _():
        start_gather(0, 0)

    @pl.when(i + 1 < nused)
    def _():
        start_gather(i + 1, 1 - slot)

    @pl.when((i >= 2) & (i - 2 < nused))
    def _():
        wait_scatter(i - 2, slot)

    @pl.when(i < nused)
    def _():
        wait_gather(i, slot)
        x = _slab_load(gbuf, pl.multiple_of(slot * slab_rows, SUBLANES), tm)
        xb = x.astype(BF16)
        scores = jax.nn.sigmoid(jnp.dot(xb, rw_ref[...], preferred_element_type=F32))
        lane = lax.broadcasted_iota(jnp.int32, scores.shape, 1)
        sa = jnp.sum(jnp.where(lane == ea_ref[i], scores, 0.0), axis=-1, keepdims=True)
        sb = jnp.sum(jnp.where(lane == eb_ref[i], scores, 0.0), axis=-1, keepdims=True)
        denom = sa + sb

        def expert(wg_ref, wu_ref, wd_ref):
            gt = jnp.dot(xb, wg_ref[...], preferred_element_type=F32)
            up = jnp.dot(xb, wu_ref[...], preferred_element_type=F32)
            hid = (jax.nn.silu(gt) * up).astype(BF16)
            return jnp.dot(hid, wd_ref[...], preferred_element_type=F32)

        f = (sa / denom) * expert(wga_ref, wua_ref, wda_ref)
        f = f + (sb / denom) * expert(wgb_ref, wub_ref, wdb_ref)
        y = _layer_norm(DEEPNORM_ALPHA * x + f, g_ref[...], b_ref[...])
        if out_slab:
            _slab_store(obuf, pl.multiple_of(slot * slab_rows, SUBLANES), y)
        else:
            obuf[slot] = y
        start_scatter(i, slot)

    @pl.when(i == nt - 1)
    def _():
        for back in (1, 0):
            tile = i - back

            @pl.when((tile >= 0) & (tile < nused))
            def _():
                wait_scatter(tile, lax.rem(tile, 2))


def _moe(tok, tile_ea, tile_eb, nvalid, nused, x1, rw_pad, wg, wu, wd, g, b, layer, tm, out_slab):
    nt = tile_ea.shape[0]
    t = x1.shape[0]
    d = SUBLANES * LANES
    ff = wg.shape[-1]
    const = lambda shape: pl.BlockSpec(shape, lambda i, *_: (0,) * len(shape))
    up_a = pl.BlockSpec((None, None, d, ff), lambda i, tok, ea, *_: (layer, ea[i], 0, 0))
    up_b = pl.BlockSpec((None, None, d, ff), lambda i, tok, ea, eb, *_: (layer, eb[i], 0, 0))
    dn_a = pl.BlockSpec((None, None, ff, d), lambda i, tok, ea, *_: (layer, ea[i], 0, 0))
    dn_b = pl.BlockSpec((None, None, ff, d), lambda i, tok, ea, eb, *_: (layer, eb[i], 0, 0))
    if out_slab:
        out_shape = jax.ShapeDtypeStruct((t, SUBLANES, LANES), F32)
        obuf = pltpu.VMEM((2 * tm * SUBLANES, LANES), F32)
    else:
        out_shape = jax.ShapeDtypeStruct((t, d), F32)
        obuf = pltpu.VMEM((2, tm, d), F32)
    return pl.pallas_call(
        functools.partial(_moe_kernel, tm=tm, out_slab=out_slab),
        grid_spec=pltpu.PrefetchScalarGridSpec(
            num_scalar_prefetch=5,
            grid=(nt,),
            in_specs=[
                pl.BlockSpec(memory_space=pl.ANY), const((d, 128)),
                up_a, up_a, dn_a, up_b, up_b, dn_b,
                const((1, d)), const((1, d)),
            ],
            out_specs=pl.BlockSpec(memory_space=pl.ANY),
            scratch_shapes=[
                pltpu.VMEM((2 * tm * SUBLANES, LANES), F32), pltpu.SemaphoreType.DMA((2,)),
                obuf, pltpu.SemaphoreType.DMA((2,)),
            ],
        ),
        out_shape=out_shape,
        compiler_params=pltpu.CompilerParams(
            dimension_semantics=("arbitrary",), vmem_limit_bytes=V7X_VMEM_LIMIT),
        name=f"moe_{layer}",
    )(tok, tile_ea, tile_eb, nvalid, nused, x1, rw_pad, wg, wu, wd, wg, wu, wd, g, b)


def _route_and_experts(x1_slab, cls, rw_pad, wg, wu, wd, g, b, layer, tm, out_slab):
    t = cls.shape[1]
    nt = t // tm + N_CLASSES
    ntp = -(-nt // 128) * 128
    pos, tile_ea, tile_eb, nvalid, nused = _positions(cls, tm, ntp)
    tok = _invert(pos.reshape(t), nt * tm)
    per_tile = lambda a: a.reshape(ntp)[:nt]
    return _moe(tok, per_tile(tile_ea), per_tile(tile_eb), per_tile(nvalid), nused.reshape(128)[:1],
                x1_slab.reshape(t, SUBLANES, LANES), rw_pad, wg, wu, wd, g, b, layer, tm, out_slab)


def _forward(x, a_w_in, a_conv_w, a_w_out, b_w_in, b_ln_g, b_ln_b, b_ws, b_bs, b_w_out,
             router_w, router_bias, moe_w_gate, moe_w_up, moe_w_down,
             ln_mix_g, ln_mix_b, ln_ffn_g, ln_ffn_b, *, ts_conv, ts_sgu, tm):
    bn, s, d = x.shape
    assert d == SUBLANES * LANES
    t = bn * s
    rwt = jnp.concatenate(_split_bf16(router_w.T.astype(F32)), axis=0)
    rb = router_bias.astype(F32).reshape(N_EXPERTS, 1)
    rw_pad = jnp.pad(router_w, ((0, 0), (0, 128 - N_EXPERTS))).astype(BF16)
    wg = moe_w_gate.astype(BF16)
    wu = moe_w_up.astype(BF16)
    wd = moe_w_down.astype(BF16)
    row = lambda a: a.reshape(1, -1)

    x1, cls = _conv_mixer(x, a_w_in[0].astype(BF16), a_conv_w[0], a_w_out[0].astype(BF16),
                          row(ln_mix_g[0]), row(ln_mix_b[0]), rwt, rb, ts_conv)
    x2 = _route_and_experts(x1, cls, rw_pad, wg, wu, wd,
                            row(ln_ffn_g[0]), row(ln_ffn_b[0]), 0, tm, True)
    x3, cls = _sgu_mixer(x2.reshape(t * SUBLANES, LANES), b_w_in[0].astype(BF16),
                         row(b_ln_g[0]), row(b_ln_b[0]), b_ws[0], b_bs[0].T,
                         b_w_out[0].astype(BF16), row(ln_mix_g[1]), row(ln_mix_b[1]), rwt, rb,
                         ts_sgu)
    x4 = _route_and_experts(x3, cls, rw_pad, wg, wu, wd,
                            row(ln_ffn_g[1]), row(ln_ffn_b[1]), 1, tm, False)
    return x4.reshape(bn, s, d)


def kernel(x, a_w_in, a_conv_w, a_w_out, b_w_in, b_ln_g, b_ln_b, b_ws, b_bs, b_w_out, router_w, router_bias, moe_w_gate, moe_w_up, moe_w_down, ln_mix_g, ln_mix_b, ln_ffn_g, ln_ffn_b):
    return _forward(x, a_w_in, a_conv_w, a_w_out, b_w_in, b_ln_g, b_ln_b, b_ws, b_bs, b_w_out,
                    router_w, router_bias, moe_w_gate, moe_w_up, moe_w_down,
                    ln_mix_g, ln_mix_b, ln_ffn_g, ln_ffn_b,
                    ts_conv=512, ts_sgu=256, tm=256)
```

```python
import functools

import jax
import jax.numpy as jnp
from jax import lax
from jax.experimental import pallas as pl
from jax.experimental.pallas import tpu as pltpu

F32 = jnp.float32
BF16 = jnp.bfloat16

N_EXPERTS = 16
N_GROUPS = 4
EXPERTS_PER_GROUP = 4
CHUNK = 128
SGU_HEADS = 8
CONV_WIDTH = 3
DEPTH = 2
DEEPNORM_ALPHA = (2 * DEPTH) ** 0.25
LN_EPS = 1e-5

_PAIRS = ((0, 1), (0, 2), (0, 3), (1, 3), (2, 3), (2, 1))
N_PAIRS = len(_PAIRS)
N_CLASSES = N_GROUPS * N_PAIRS
CLASS_ROWS = 32
_CLASS_EA = tuple(g * EXPERTS_PER_GROUP + p[0] for g in range(N_GROUPS) for p in _PAIRS)
_CLASS_EB = tuple(g * EXPERTS_PER_GROUP + p[1] for g in range(N_GROUPS) for p in _PAIRS)

LANES = 128
SUBLANES = 8
V7X_VMEM_LIMIT = 56 * 1024 * 1024
INVERT_UNROLL = 16


def _layer_norm(y, g, b):
    mu = jnp.mean(y, axis=-1, keepdims=True)
    d = y - mu
    var = jnp.mean(d * d, axis=-1, keepdims=True)
    return d * lax.rsqrt(var + LN_EPS) * g + b


def _gelu(x):
    return 0.5 * x * (1.0 + lax.erf(x * (2.0 ** -0.5)))


def _slab_load(ref, base, rows):
    return jnp.concatenate(
        [ref[pl.ds(base + s, rows, stride=SUBLANES), :] for s in range(SUBLANES)], axis=1)


def _slab_store(ref, base, val):
    rows = val.shape[0]
    for s in range(SUBLANES):
        ref[pl.ds(base + s, rows, stride=SUBLANES), :] = val[:, s * LANES:(s + 1) * LANES]


def _split_bf16(a):
    hi = a.astype(BF16)
    return hi, (a - hi.astype(F32)).astype(BF16)


def _route_class(x1, rw_split, rb):
    x_hi, x_lo = _split_bf16(x1)
    nt_dims = (((1,), (1,)), ((), ()))
    by_hi = lax.dot_general(rw_split, x_hi, nt_dims, preferred_element_type=F32)
    by_lo = lax.dot_general(rw_split[:N_EXPERTS, :], x_lo, nt_dims, preferred_element_type=F32)
    logits = by_hi[:N_EXPERTS, :] + (by_hi[N_EXPERTS:, :] + by_lo)
    biased = jax.nn.sigmoid(logits) + rb
    v = [biased[e:e + 1, :] for e in range(N_EXPERTS)]
    best_score = None
    best_cls = None
    for g in range(N_GROUPS):
        vg = v[g * EXPERTS_PER_GROUP:(g + 1) * EXPERTS_PER_GROUP]
        sel = []
        for i in range(EXPERTS_PER_GROUP):
            rank = jnp.zeros_like(vg[i], dtype=jnp.int32)
            for j in range(EXPERTS_PER_GROUP):
                if j == i:
                    continue
                beats = (vg[j] >= vg[i]) if j < i else (vg[j] > vg[i])
                rank = rank + beats.astype(jnp.int32)
            sel.append(rank < 2)
        top_sum = None
        cls_g = jnp.zeros_like(vg[0], dtype=jnp.int32)
        for p, (a, b) in enumerate(_PAIRS):
            is_pair = sel[a] & sel[b]
            cls_g = jnp.where(is_pair, g * N_PAIRS + p, cls_g)
            pair_sum = vg[min(a, b)] + vg[max(a, b)]
            top_sum = jnp.where(is_pair, pair_sum, 0.0 if top_sum is None else top_sum)
        if best_score is None:
            best_score, best_cls = top_sum, cls_g
        else:
            better = top_sum > best_score
            best_score = jnp.where(better, top_sum, best_score)
            best_cls = jnp.where(better, cls_g, best_cls)
    return best_cls


def _conv_mixer_kernel(x_ref, w_in_ref, cw_ref, w_out_ref, g_ref, b_ref, rwt_ref, rb_ref,
                       x1_ref, cls_ref, carry_ref):
    ts, d = x_ref.shape

    @pl.when(pl.program_id(1) == 0)
    def _():
        carry_ref[...] = jnp.zeros_like(carry_ref)

    x = x_ref[...]
    xb = x.astype(BF16)
    cg = jnp.dot(xb, w_in_ref[:, d:2 * d], preferred_element_type=F32)
    h = jnp.dot(xb, w_in_ref[:, 2 * d:3 * d], preferred_element_type=F32)
    ch = cg * h
    prev2 = carry_ref[6:7, :]
    prev1 = carry_ref[7:8, :]
    rows = lax.broadcasted_iota(jnp.int32, (ts, 1), 0)
    ch1 = jnp.where(rows == 0, prev1, pltpu.roll(ch, 1, 0))
    ch2 = jnp.where(rows == 0, prev2, jnp.where(rows == 1, prev1, pltpu.roll(ch, 2, 0)))
    z = cw_ref[0:1, :] * ch2 + cw_ref[1:2, :] * ch1 + cw_ref[2:3, :] * ch
    carry_ref[...] = ch[ts - 8:ts, :]
    bg = jnp.dot(xb, w_in_ref[:, 0:d], preferred_element_type=F32)
    m = jnp.dot((bg * z).astype(BF16), w_out_ref[...], preferred_element_type=F32)
    x1 = _layer_norm(DEEPNORM_ALPHA * x + m, g_ref[...], b_ref[...])
    _slab_store(x1_ref, 0, x1)
    cls_ref[...] = _route_class(x1, rwt_ref[...], rb_ref[...])


def _conv_mixer(x, w_in, conv_w, w_out, ln_g, ln_b, rwt, rb, ts):
    bn, s, d = x.shape
    ns = s // ts
    const = lambda shape: pl.BlockSpec(shape, lambda b, i: (0,) * len(shape))
    return pl.pallas_call(
        _conv_mixer_kernel,
        grid=(bn, ns),
        in_specs=[
            pl.BlockSpec((None, ts, d), lambda b, i: (b, i, 0)),
            const((d, 3 * d)), const((CONV_WIDTH, d)), const((d, d)),
            const((1, d)), const((1, d)), const((2 * N_EXPERTS, d)), const((N_EXPERTS, 1)),
        ],
        out_specs=[
            pl.BlockSpec((ts * SUBLANES, LANES), lambda b, i: (b * ns + i, 0)),
            pl.BlockSpec((1, ts), lambda b, i: (0, b * ns + i)),
        ],
        out_shape=[
            jax.ShapeDtypeStruct((bn * s * SUBLANES, LANES), F32),
            jax.ShapeDtypeStruct((1, bn * s), jnp.int32),
        ],
        scratch_shapes=[pltpu.VMEM((8, d), F32)],
        compiler_params=pltpu.CompilerParams(
            dimension_semantics=("arbitrary", "arbitrary"), vmem_limit_bytes=V7X_VMEM_LIMIT),
        name="conv_mixer",
    )(x, w_in, conv_w, w_out, ln_g, ln_b, rwt, rb)


def _sgu_mixer_kernel(xs_ref, w_in_ref, lng_ref, lnb_ref, ws_ref, bst_ref, w_out_ref,
                      g_ref, b_ref, rwt_ref, rb_ref, x1_ref, cls_ref, v_ref, gate_ref):
    ts, width = v_ref.shape
    hd = width // SGU_HEADS
    x = _slab_load(xs_ref, 0, ts)
    xb = x.astype(BF16)
    v = _gelu(jnp.dot(xb, w_in_ref[:, width:2 * width], preferred_element_type=F32))
    v_ref[...] = _layer_norm(v, lng_ref[...], lnb_ref[...]).astype(BF16)
    r_i = lax.broadcasted_iota(jnp.int32, (CHUNK, CHUNK), 0)
    c_i = lax.broadcasted_iota(jnp.int32, (CHUNK, CHUNK), 1)
    causal = r_i >= c_i
    for h in range(SGU_HEADS):
        cols = slice(h * hd, (h + 1) * hd)
        u_h = _gelu(jnp.dot(xb, w_in_ref[:, cols], preferred_element_type=F32))
        w_h = jnp.where(causal, ws_ref[h], 0.0).astype(BF16)
        bias = bst_ref[:, h:h + 1]
        for c in range(ts // CHUNK):
            rws = slice(c * CHUNK, (c + 1) * CHUNK)
            mixed = jnp.dot(w_h, v_ref[rws, cols], preferred_element_type=F32) + bias
            gate_ref[rws, cols] = (u_h[rws, :] * mixed).astype(BF16)
    m = jnp.dot(gate_ref[...], w_out_ref[...], preferred_element_type=F32)
    x1 = _layer_norm(DEEPNORM_ALPHA * x + m, g_ref[...], b_ref[...])
    _slab_store(x1_ref, 0, x1)
    cls_ref[...] = _route_class(x1, rwt_ref[...], rb_ref[...])


def _sgu_mixer(xs, t, w_in, ln_g, ln_b, ws, bst, w_out, g, b, rwt, rb, ts):
    width, d = w_out.shape
    const = lambda shape: pl.BlockSpec(shape, lambda i: (0,) * len(shape))
    return pl.pallas_call(
        _sgu_mixer_kernel,
        grid=(t // ts,),
        in_specs=[
            pl.BlockSpec((ts * SUBLANES, LANES), lambda i: (i, 0)),
            const((d, 2 * width)), const((1, width)), const((1, width)),
            const((SGU_HEADS, CHUNK, CHUNK)), const((CHUNK, SGU_HEADS)), const((width, d)),
            const((1, d)), const((1, d)), const((2 * N_EXPERTS, d)), const((N_EXPERTS, 1)),
        ],
        out_specs=[
            pl.BlockSpec((ts * SUBLANES, LANES), lambda i: (i, 0)),
            pl.BlockSpec((1, ts), lambda i: (0, i)),
        ],
        out_shape=[
            jax.ShapeDtypeStruct((t * SUBLANES, LANES), F32),
            jax.ShapeDtypeStruct((1, t), jnp.int32),
        ],
        scratch_shapes=[pltpu.VMEM((ts, width), BF16), pltpu.VMEM((ts, width), BF16)],
        compiler_params=pltpu.CompilerParams(
            dimension_semantics=("arbitrary",), vmem_limit_bytes=V7X_VMEM_LIMIT),
        name="sgu_mixer",
    )(xs, w_in, ln_g, ln_b, ws, bst, w_out, g, b, rwt, rb)


def _positions_kernel(cls_ref, pos_ref, tile_ea_ref, tile_eb_ref, nused_ref, incl_ref, *, tm):
    t = cls_ref.shape[1]
    ntp = tile_ea_ref.shape[1]
    lanes = 256
    cls = cls_ref[...]
    crow = lax.broadcasted_iota(jnp.int32, (CLASS_ROWS, t), 0)
    onehot = (crow == cls).astype(F32)
    k_i = lax.broadcasted_iota(jnp.int32, (lanes, lanes), 0)
    j_i = lax.broadcasted_iota(jnp.int32, (lanes, lanes), 1)
    upper = (k_i <= j_i).astype(BF16)
    count = jnp.zeros((CLASS_ROWS, 1), F32)
    for c in range(t // lanes):
        sl = slice(c * lanes, (c + 1) * lanes)
        inc = jnp.dot(onehot[:, sl].astype(BF16), upper, preferred_element_type=F32) + count
        incl_ref[:, sl] = inc
        count = inc[:, lanes - 1:lanes]
    rank = jnp.sum(onehot * incl_ref[...], axis=0, keepdims=True) - 1.0
    ntile = jnp.floor((count + (tm - 1)) * (1.0 / tm))
    srow = lax.broadcasted_iota(jnp.int32, (CLASS_ROWS, 1), 0)
    tstart = jnp.zeros((CLASS_ROWS, 1), F32)
    for c in range(N_CLASSES):
        tstart = tstart + jnp.where(srow > c, ntile[c:c + 1, :], 0.0)
    tend = tstart + ntile
    pos = jnp.sum(onehot * (tstart * tm), axis=0, keepdims=True) + rank
    pos_ref[...] = pos.astype(jnp.int32)
    nused = jnp.max(jnp.where(srow < N_CLASSES, tend, 0.0), axis=0, keepdims=True)
    last_cls = jnp.max(jnp.where((ntile > 0) & (srow < N_CLASSES), srow, 0), axis=0, keepdims=True)
    tile_i = lax.broadcasted_iota(jnp.int32, (CLASS_ROWS, ntp), 1).astype(F32)
    is_cls = lax.broadcasted_iota(jnp.int32, (CLASS_ROWS, ntp), 0) < N_CLASSES
    done = (tile_i >= tend) & is_cls
    tcls = jnp.minimum(jnp.sum(done.astype(jnp.int32), axis=0, keepdims=True), last_cls)
    ea = jnp.zeros((1, ntp), jnp.int32)
    eb = jnp.zeros((1, ntp), jnp.int32)
    for c in range(N_CLASSES):
        ea = jnp.where(tcls == c, _CLASS_EA[c], ea)
        eb = jnp.where(tcls == c, _CLASS_EB[c], eb)
    tile_ea_ref[...] = ea
    tile_eb_ref[...] = eb
    nused_ref[...] = jnp.broadcast_to(nused, nused_ref.shape).astype(jnp.int32)


def _positions(cls, tm, ntp):
    t = cls.shape[1]
    return pl.pallas_call(
        functools.partial(_positions_kernel, tm=tm),
        out_shape=[
            jax.ShapeDtypeStruct((1, t), jnp.int32),
            jax.ShapeDtypeStruct((1, ntp), jnp.int32),
            jax.ShapeDtypeStruct((1, ntp), jnp.int32),
            jax.ShapeDtypeStruct((1, 128), jnp.int32),
        ],
        scratch_shapes=[pltpu.VMEM((CLASS_ROWS, t), F32)],
        compiler_params=pltpu.CompilerParams(vmem_limit_bytes=V7X_VMEM_LIMIT),
        name="positions",
    )(cls)


def _invert_kernel(pos_ref, tokg_ref, toks_ref, *, tm):
    t = pos_ref.shape[0]

    def fill_g(j, carry):
        for u in range(INVERT_UNROLL):
            tokg_ref[j * INVERT_UNROLL + u] = 0
        return carry

    lax.fori_loop(0, tokg_ref.shape[0] // INVERT_UNROLL, fill_g, 0)

    def fill_s(j, carry):
        for u in range(INVERT_UNROLL):
            k = j * INVERT_UNROLL + u
            toks_ref[k] = t + lax.rem(k, 2 * tm)
        return carry

    lax.fori_loop(0, toks_ref.shape[0] // INVERT_UNROLL, fill_s, 0)

    def put(j, carry):
        k0 = j * INVERT_UNROLL
        dst = [pos_ref[k0 + u] for u in range(INVERT_UNROLL)]
        for u in range(INVERT_UNROLL):
            tokg_ref[dst[u]] = k0 + u
            toks_ref[2 * tm + dst[u]] = k0 + u
        return carry

    lax.fori_loop(0, t // INVERT_UNROLL, put, 0)


def _invert(pos, nt, tm):
    return pl.pallas_call(
        functools.partial(_invert_kernel, tm=tm),
        in_specs=[pl.BlockSpec(memory_space=pltpu.SMEM)],
        out_specs=[pl.BlockSpec(memory_space=pltpu.SMEM), pl.BlockSpec(memory_space=pltpu.SMEM)],
        out_shape=[jax.ShapeDtypeStruct(((nt + 1) * tm,), jnp.int32),
                   jax.ShapeDtypeStruct(((nt + 2) * tm,), jnp.int32)],
        name="invert_positions",
    )(pos)


def _moe_kernel(tokg_ref, toks_ref, ea_ref, eb_ref, nused_ref, x_hbm, rw_ref, wga_ref, wua_ref,
                wda_ref, wgb_ref, wub_ref, wdb_ref, g_ref, b_ref, out_hbm, gbuf, gsem, obuf, osem,
                *, tm, out_slab):
    i = pl.program_id(0)
    nused = nused_ref[0]
    slot = lax.rem(i, 2)
    other = 1 - slot
    slab_rows = tm * SUBLANES

    def slab_of(buf, sl, r):
        return buf.at[pl.ds(pl.multiple_of((sl * tm + r) * SUBLANES, SUBLANES), SUBLANES), :]

    def start_gather(tile, sl):
        for r in range(tm):
            pltpu.make_async_copy(x_hbm.at[tokg_ref[tile * tm + r]], slab_of(gbuf, sl, r),
                                  gsem.at[sl]).start()

    def wait_gather(sl):
        whole = gbuf.at[pl.ds(pl.multiple_of(sl * slab_rows, SUBLANES), slab_rows), :]
        pltpu.make_async_copy(whole, whole, gsem.at[sl]).wait()

    def start_scatter(tile, sl):
        for r in range(tm):
            tok = toks_ref[(tile + 2) * tm + r]
            if out_slab:
                src, dst = slab_of(obuf, sl, r), out_hbm.at[tok]
            else:
                src, dst = obuf.at[sl, pl.ds(r, 1), :], out_hbm.at[pl.ds(tok, 1), :]
            pltpu.make_async_copy(src, dst, osem.at[sl]).start()

    def wait_scatter(sl):
        if out_slab:
            whole = obuf.at[pl.ds(pl.multiple_of(sl * slab_rows, SUBLANES), slab_rows), :]
        else:
            whole = obuf.at[sl]
        pltpu.make_async_copy(whole, whole, osem.at[sl]).wait()

    @pl.when(i == 0)
    def _():
        obuf[...] = jnp.zeros_like(obuf)
        start_gather(0, 0)
        start_scatter(-2, 0)

    @pl.when(i < nused)
    def _():
        wait_gather(slot)
        x = _slab_load(gbuf, pl.multiple_of(slot * slab_rows, SUBLANES), tm)
        xb = x.astype(BF16)
        start_gather(i + 1, other)
        start_scatter(i - 1, other)
        scores = jax.nn.sigmoid(jnp.dot(xb, rw_ref[...], preferred_element_type=F32))
        lane = lax.broadcasted_iota(jnp.int32, scores.shape, 1)
        sa = jnp.sum(jnp.where(lane == ea_ref[i], scores, 0.0), axis=-1, keepdims=True)
        sb = jnp.sum(jnp.where(lane == eb_ref[i], scores, 0.0), axis=-1, keepdims=True)
        denom = sa + sb

        def expert(wg_ref, wu_ref, wd_ref):
            gt = jnp.dot(xb, wg_ref[...], preferred_element_type=F32)
            up = jnp.dot(xb, wu_ref[...], preferred_element_type=F32)
            hid = (jax.nn.silu(gt) * up).astype(BF16)
            return jnp.dot(hid, wd_ref[...], preferred_element_type=F32)

        f = (sa / denom) * expert(wga_ref, wua_ref, wda_ref)
        f = f + (sb / denom) * expert(wgb_ref, wub_ref, wdb_ref)
        y = _layer_norm(DEEPNORM_ALPHA * x + f, g_ref[...], b_ref[...])
        wait_scatter(slot)
        if out_slab:
            _slab_store(obuf, pl.multiple_of(slot * slab_rows, SUBLANES), y)
        else:
            obuf[slot] = y

    @pl.when(i == nused)
    def _():
        wait_gather(slot)
        start_scatter(i - 1, other)
        wait_scatter(slot)
        wait_scatter(other)


def _moe(tokg, toks, tile_ea, tile_eb, nused, x1, rw_pad, wg, wu, wd, g, b, layer, tm, out_slab):
    steps = tile_ea.shape[0]
    t = x1.shape[0]
    d = SUBLANES * LANES
    ff = wg.shape[-1]
    const = lambda shape: pl.BlockSpec(shape, lambda i, *_: (0,) * len(shape))
    up_a = pl.BlockSpec((None, None, d, ff), lambda i, tg, ts, ea, *_: (layer, ea[i], 0, 0))
    up_b = pl.BlockSpec((None, None, d, ff), lambda i, tg, ts, ea, eb, *_: (layer, eb[i], 0, 0))
    dn_a = pl.BlockSpec((None, None, ff, d), lambda i, tg, ts, ea, *_: (layer, ea[i], 0, 0))
    dn_b = pl.BlockSpec((None, None, ff, d), lambda i, tg, ts, ea, eb, *_: (layer, eb[i], 0, 0))
    rows_out = t + 2 * tm
    if out_slab:
        out_shape = jax.ShapeDtypeStruct((rows_out, SUBLANES, LANES), F32)
        obuf = pltpu.VMEM((2 * tm * SUBLANES, LANES), F32)
    else:
        out_shape = jax.ShapeDtypeStruct((rows_out, d), F32)
        obuf = pltpu.VMEM((2, tm, d), F32)
    return pl.pallas_call(
        functools.partial(_moe_kernel, tm=tm, out_slab=out_slab),
        grid_spec=pltpu.PrefetchScalarGridSpec(
            num_scalar_prefetch=5,
            grid=(steps,),
            in_specs=[
                pl.BlockSpec(memory_space=pl.ANY), const((d, 128)),
                up_a, up_a, dn_a, up_b, up_b, dn_b,
                const((1, d)), const((1, d)),
            ],
            out_specs=pl.BlockSpec(memory_space=pl.ANY),
            scratch_shapes=[
                pltpu.VMEM((2 * tm * SUBLANES, LANES), F32), pltpu.SemaphoreType.DMA((2,)),
                obuf, pltpu.SemaphoreType.DMA((2,)),
            ],
        ),
        out_shape=out_shape,
        compiler_params=pltpu.CompilerParams(
            dimension_semantics=("arbitrary",), vmem_limit_bytes=V7X_VMEM_LIMIT),
        name=f"moe_{layer}",
    )(tokg, toks, tile_ea, tile_eb, nused, x1, rw_pad, wg, wu, wd, wg, wu, wd, g, b)


def _route_and_experts(x1_slab, cls, rw_pad, wg, wu, wd, g, b, layer, tm, out_slab):
    t = cls.shape[1]
    nt = t // tm + N_CLASSES
    ntp = -(-(nt + 1) // 128) * 128
    pos, tile_ea, tile_eb, nused = _positions(cls, tm, ntp)
    tokg, toks = _invert(pos.reshape(t), nt, tm)
    per_step = lambda a: a.reshape(ntp)[:nt + 1]
    return _moe(tokg, toks, per_step(tile_ea), per_step(tile_eb), nused.reshape(128)[:1],
                x1_slab.reshape(-1, SUBLANES, LANES), rw_pad, wg, wu, wd, g, b, layer, tm, out_slab)


def _forward(x, a_w_in, a_conv_w, a_w_out, b_w_in, b_ln_g, b_ln_b, b_ws, b_bs, b_w_out,
             router_w, router_bias, moe_w_gate, moe_w_up, moe_w_down,
             ln_mix_g, ln_mix_b, ln_ffn_g, ln_ffn_b, *, ts_conv, ts_sgu, tm):
    bn, s, d = x.shape
    assert d == SUBLANES * LANES
    t = bn * s
    rwt = jnp.concatenate(_split_bf16(router_w.T.astype(F32)), axis=0)
    rb = router_bias.astype(F32).reshape(N_EXPERTS, 1)
    rw_pad = jnp.pad(router_w, ((0, 0), (0, 128 - N_EXPERTS))).astype(BF16)
    wg = moe_w_gate.astype(BF16)
    wu = moe_w_up.astype(BF16)
    wd = moe_w_down.astype(BF16)
    row = lambda a: a.reshape(1, -1)

    x1, cls = _conv_mixer(x, a_w_in[0].astype(BF16), a_conv_w[0], a_w_out[0].astype(BF16),
                          row(ln_mix_g[0]), row(ln_mix_b[0]), rwt, rb, ts_conv)
    x2 = _route_and_experts(x1, cls, rw_pad, wg, wu, wd,
                            row(ln_ffn_g[0]), row(ln_ffn_b[0]), 0, tm, True)
    x3, cls = _sgu_mixer(x2.reshape(-1, LANES), t, b_w_in[0].astype(BF16),
                         row(b_ln_g[0]), row(b_ln_b[0]), b_ws[0], b_bs[0].T,
                         b_w_out[0].astype(BF16), row(ln_mix_g[1]), row(ln_mix_b[1]), rwt, rb,
                         ts_sgu)
    x4 = _route_and_experts(x3, cls, rw_pad, wg, wu, wd,
                            row(ln_ffn_g[1]), row(ln_ffn_b[1]), 1, tm, False)
    return x4[:t].reshape(bn, s, d)


def kernel(x, a_w_in, a_conv_w, a_w_out, b_w_in, b_ln_g, b_ln_b, b_ws, b_bs, b_w_out, router_w, router_bias, moe_w_gate, moe_w_up, moe_w_down, ln_mix_g, ln_mix_b, ln_ffn_g, ln_ffn_b):
    return _forward(x, a_w_in, a_conv_w, a_w_out, b_w_in, b_ln_g, b_ln_b, b_ws, b_bs, b_w_out,
                    router_w, router_bias, moe_w_gate, moe_w_up, moe_w_down,
                    ln_mix_g, ln_mix_b, ln_ffn_g, ln_ffn_b,
                    ts_conv=512, ts_sgu=256, tm=256)
```

```python
import functools

import jax
import jax.numpy as jnp
from jax import lax
from jax.experimental import pallas as pl
from jax.experimental.pallas import tpu as pltpu

F32 = jnp.float32
BF16 = jnp.bfloat16

N_EXPERTS = 16
N_GROUPS = 4
EXPERTS_PER_GROUP = 4
CHUNK = 128
SGU_HEADS = 8
CONV_WIDTH = 3
DEPTH = 2
DEEPNORM_ALPHA = (2 * DEPTH) ** 0.25
LN_EPS = 1e-5

_PAIRS = ((0, 1), (0, 2), (0, 3), (1, 3), (2, 3), (2, 1))
N_PAIRS = len(_PAIRS)
N_CLASSES = N_GROUPS * N_PAIRS
CLASS_ROWS = 32
_CLASS_EA = tuple(g * EXPERTS_PER_GROUP + p[0] for g in range(N_GROUPS) for p in _PAIRS)
_CLASS_EB = tuple(g * EXPERTS_PER_GROUP + p[1] for g in range(N_GROUPS) for p in _PAIRS)

LANES = 128
SUBLANES = 8
V7X_VMEM_LIMIT = 56 * 1024 * 1024
INVERT_UNROLL = 16


def _layer_norm(y, g, b):
    mu = jnp.mean(y, axis=-1, keepdims=True)
    d = y - mu
    var = jnp.mean(d * d, axis=-1, keepdims=True)
    return d * lax.rsqrt(var + LN_EPS) * g + b


def _gelu(x):
    return 0.5 * x * (1.0 + lax.erf(x * (2.0 ** -0.5)))


def _slab_load(ref, base, rows):
    return jnp.concatenate(
        [ref[pl.ds(base + s, rows, stride=SUBLANES), :] for s in range(SUBLANES)], axis=1)


def _slab_store(ref, base, val):
    rows = val.shape[0]
    for s in range(SUBLANES):
        ref[pl.ds(base + s, rows, stride=SUBLANES), :] = val[:, s * LANES:(s + 1) * LANES]


def _zero_row(tile, width):
    bits = lax.bitcast_convert_type(tile[0:1, :], jnp.uint32)
    zero = lax.shift_right_logical(lax.shift_right_logical(bits, jnp.uint32(16)), jnp.uint32(16))
    return jnp.concatenate([zero.astype(F32)] * (width // LANES), axis=1)


def _split_bf16(a):
    hi = a.astype(BF16)
    return hi, (a - hi.astype(F32)).astype(BF16)


def _route_class(x1, rw_split, rb):
    x_hi, x_lo = _split_bf16(x1)
    nt_dims = (((1,), (1,)), ((), ()))
    by_hi = lax.dot_general(rw_split, x_hi, nt_dims, preferred_element_type=F32)
    by_lo = lax.dot_general(rw_split[:N_EXPERTS, :], x_lo, nt_dims, preferred_element_type=F32)
    logits = by_hi[:N_EXPERTS, :] + (by_hi[N_EXPERTS:, :] + by_lo)
    biased = jax.nn.sigmoid(logits) + rb
    v = [biased[e:e + 1, :] for e in range(N_EXPERTS)]
    best_score = None
    best_cls = None
    for g in range(N_GROUPS):
        vg = v[g * EXPERTS_PER_GROUP:(g + 1) * EXPERTS_PER_GROUP]
        sel = []
        for i in range(EXPERTS_PER_GROUP):
            rank = jnp.zeros_like(vg[i], dtype=jnp.int32)
            for j in range(EXPERTS_PER_GROUP):
                if j == i:
                    continue
                beats = (vg[j] >= vg[i]) if j < i else (vg[j] > vg[i])
                rank = rank + beats.astype(jnp.int32)
            sel.append(rank < 2)
        top_sum = None
        cls_g = jnp.zeros_like(vg[0], dtype=jnp.int32)
        for p, (a, b) in enumerate(_PAIRS):
            is_pair = sel[a] & sel[b]
            cls_g = jnp.where(is_pair, g * N_PAIRS + p, cls_g)
            pair_sum = vg[min(a, b)] + vg[max(a, b)]
            top_sum = jnp.where(is_pair, pair_sum, 0.0 if top_sum is None else top_sum)
        if best_score is None:
            best_score, best_cls = top_sum, cls_g
        else:
            better = top_sum > best_score
            best_score = jnp.where(better, top_sum, best_score)
            best_cls = jnp.where(better, cls_g, best_cls)
    return best_cls


def _conv_mixer_kernel(x_ref, w_in_ref, cw_ref, w_out_ref, g_ref, b_ref, rwt_ref, rb_ref,
                       x1_ref, cls_ref, carry_ref):
    ts, d = x_ref.shape

    @pl.when(pl.program_id(1) == 0)
    def _():
        carry_ref[...] = jnp.zeros_like(carry_ref)

    x = x_ref[...]
    xb = x.astype(BF16)
    cg = jnp.dot(xb, w_in_ref[:, d:2 * d], preferred_element_type=F32)
    h = jnp.dot(xb, w_in_ref[:, 2 * d:3 * d], preferred_element_type=F32)
    ch = cg * h
    prev2 = carry_ref[6:7, :]
    prev1 = carry_ref[7:8, :]
    rows = lax.broadcasted_iota(jnp.int32, (ts, 1), 0)
    ch1 = jnp.where(rows == 0, prev1, pltpu.roll(ch, 1, 0))
    ch2 = jnp.where(rows == 0, prev2, jnp.where(rows == 1, prev1, pltpu.roll(ch, 2, 0)))
    z = cw_ref[0:1, :] * ch2 + cw_ref[1:2, :] * ch1 + cw_ref[2:3, :] * ch
    carry_ref[...] = ch[ts - 8:ts, :]
    bg = jnp.dot(xb, w_in_ref[:, 0:d], preferred_element_type=F32)
    m = jnp.dot((bg * z).astype(BF16), w_out_ref[...], preferred_element_type=F32)
    x1 = _layer_norm(DEEPNORM_ALPHA * x + m, g_ref[...], b_ref[...])
    _slab_store(x1_ref, 0, x1)
    cls_ref[...] = _route_class(x1, rwt_ref[...], rb_ref[...])


def _conv_mixer(x, w_in, conv_w, w_out, ln_g, ln_b, rwt, rb, ts):
    bn, s, d = x.shape
    ns = s // ts
    const = lambda shape: pl.BlockSpec(shape, lambda b, i: (0,) * len(shape))
    return pl.pallas_call(
        _conv_mixer_kernel,
        grid=(bn, ns),
        in_specs=[
            pl.BlockSpec((None, ts, d), lambda b, i: (b, i, 0)),
            const((d, 3 * d)), const((CONV_WIDTH, d)), const((d, d)),
            const((1, d)), const((1, d)), const((2 * N_EXPERTS, d)), const((N_EXPERTS, 1)),
        ],
        out_specs=[
            pl.BlockSpec((ts * SUBLANES, LANES), lambda b, i: (b * ns + i, 0)),
            pl.BlockSpec((1, ts), lambda b, i: (0, b * ns + i)),
        ],
        out_shape=[
            jax.ShapeDtypeStruct((bn * s * SUBLANES, LANES), F32),
            jax.ShapeDtypeStruct((1, bn * s), jnp.int32),
        ],
        scratch_shapes=[pltpu.VMEM((8, d), F32)],
        compiler_params=pltpu.CompilerParams(
            dimension_semantics=("arbitrary", "arbitrary"), vmem_limit_bytes=V7X_VMEM_LIMIT),
        name="conv_mixer",
    )(x, w_in, conv_w, w_out, ln_g, ln_b, rwt, rb)


def _sgu_mixer_kernel(xs_ref, w_in_ref, lng_ref, lnb_ref, ws_ref, bst_ref, w_out_ref,
                      g_ref, b_ref, rwt_ref, rb_ref, x1_ref, cls_ref, v_ref, gate_ref):
    ts, width = v_ref.shape
    hd = width // SGU_HEADS
    x = _slab_load(xs_ref, 0, ts)
    xb = x.astype(BF16)
    v = _gelu(jnp.dot(xb, w_in_ref[:, width:2 * width], preferred_element_type=F32))
    v_ref[...] = _layer_norm(v, lng_ref[...], lnb_ref[...]).astype(BF16)
    r_i = lax.broadcasted_iota(jnp.int32, (CHUNK, CHUNK), 0)
    c_i = lax.broadcasted_iota(jnp.int32, (CHUNK, CHUNK), 1)
    causal = r_i >= c_i
    for h in range(SGU_HEADS):
        cols = slice(h * hd, (h + 1) * hd)
        u_h = _gelu(jnp.dot(xb, w_in_ref[:, cols], preferred_element_type=F32))
        w_h = jnp.where(causal, ws_ref[h], 0.0).astype(BF16)
        bias = bst_ref[:, h:h + 1]
        for c in range(ts // CHUNK):
            rws = slice(c * CHUNK, (c + 1) * CHUNK)
            mixed = jnp.dot(w_h, v_ref[rws, cols], preferred_element_type=F32) + bias
            gate_ref[rws, cols] = (u_h[rws, :] * mixed).astype(BF16)
    m = jnp.dot(gate_ref[...], w_out_ref[...], preferred_element_type=F32)
    x1 = _layer_norm(DEEPNORM_ALPHA * x + m, g_ref[...], b_ref[...])
    _slab_store(x1_ref, 0, x1)
    cls_ref[...] = _route_class(x1, rwt_ref[...], rb_ref[...])


def _sgu_mixer(xs, t, w_in, ln_g, ln_b, ws, bst, w_out, g, b, rwt, rb, ts):
    width, d = w_out.shape
    const = lambda shape: pl.BlockSpec(shape, lambda i: (0,) * len(shape))
    return pl.pallas_call(
        _sgu_mixer_kernel,
        grid=(t // ts,),
        in_specs=[
            pl.BlockSpec((ts * SUBLANES, LANES), lambda i: (i, 0)),
            const((d, 2 * width)), const((1, width)), const((1, width)),
            const((SGU_HEADS, CHUNK, CHUNK)), const((CHUNK, SGU_HEADS)), const((width, d)),
            const((1, d)), const((1, d)), const((2 * N_EXPERTS, d)), const((N_EXPERTS, 1)),
        ],
        out_specs=[
            pl.BlockSpec((ts * SUBLANES, LANES), lambda i: (i, 0)),
            pl.BlockSpec((1, ts), lambda i: (0, i)),
        ],
        out_shape=[
            jax.ShapeDtypeStruct((t * SUBLANES, LANES), F32),
            jax.ShapeDtypeStruct((1, t), jnp.int32),
        ],
        scratch_shapes=[pltpu.VMEM((ts, width), BF16), pltpu.VMEM((ts, width), BF16)],
        compiler_params=pltpu.CompilerParams(
            dimension_semantics=("arbitrary",), vmem_limit_bytes=V7X_VMEM_LIMIT),
        name="sgu_mixer",
    )(xs, w_in, ln_g, ln_b, ws, bst, w_out, g, b, rwt, rb)


def _positions_kernel(cls_ref, pos_ref, tile_ea_ref, tile_eb_ref, nused_ref, incl_ref, *, tm):
    t = cls_ref.shape[1]
    ntp = tile_ea_ref.shape[1]
    lanes = 256
    cls = cls_ref[...]
    crow = lax.broadcasted_iota(jnp.int32, (CLASS_ROWS, t), 0)
    onehot = (crow == cls).astype(F32)
    k_i = lax.broadcasted_iota(jnp.int32, (lanes, lanes), 0)
    j_i = lax.broadcasted_iota(jnp.int32, (lanes, lanes), 1)
    upper = (k_i <= j_i).astype(BF16)
    count = jnp.zeros((CLASS_ROWS, 1), F32)
    for c in range(t // lanes):
        sl = slice(c * lanes, (c + 1) * lanes)
        inc = jnp.dot(onehot[:, sl].astype(BF16), upper, preferred_element_type=F32) + count
        incl_ref[:, sl] = inc
        count = inc[:, lanes - 1:lanes]
    rank = jnp.sum(onehot * incl_ref[...], axis=0, keepdims=True) - 1.0
    ntile = jnp.floor((count + (tm - 1)) * (1.0 / tm))
    srow = lax.broadcasted_iota(jnp.int32, (CLASS_ROWS, 1), 0)
    tstart = jnp.zeros((CLASS_ROWS, 1), F32)
    for c in range(N_CLASSES):
        tstart = tstart + jnp.where(srow > c, ntile[c:c + 1, :], 0.0)
    tend = tstart + ntile
    pos = jnp.sum(onehot * (tstart * tm), axis=0, keepdims=True) + rank
    pos_ref[...] = pos.astype(jnp.int32)
    nused = jnp.max(jnp.where(srow < N_CLASSES, tend, 0.0), axis=0, keepdims=True)
    last_cls = jnp.max(jnp.where((ntile > 0) & (srow < N_CLASSES), srow, 0), axis=0, keepdims=True)
    tile_i = lax.broadcasted_iota(jnp.int32, (CLASS_ROWS, ntp), 1).astype(F32)
    is_cls = lax.broadcasted_iota(jnp.int32, (CLASS_ROWS, ntp), 0) < N_CLASSES
    done = (tile_i >= tend) & is_cls
    tcls = jnp.minimum(jnp.sum(done.astype(jnp.int32), axis=0, keepdims=True), last_cls)
    ea = jnp.zeros((1, ntp), jnp.int32)
    eb = jnp.zeros((1, ntp), jnp.int32)
    for c in range(N_CLASSES):
        ea = jnp.where(tcls == c, _CLASS_EA[c], ea)
        eb = jnp.where(tcls == c, _CLASS_EB[c], eb)
    tile_ea_ref[...] = ea
    tile_eb_ref[...] = eb
    nused_ref[...] = jnp.broadcast_to(nused, nused_ref.shape).astype(jnp.int32)


def _positions(cls, tm, ntp):
    t = cls.shape[1]
    return pl.pallas_call(
        functools.partial(_positions_kernel, tm=tm),
        out_shape=[
            jax.ShapeDtypeStruct((1, t), jnp.int32),
            jax.ShapeDtypeStruct((1, ntp), jnp.int32),
            jax.ShapeDtypeStruct((1, ntp), jnp.int32),
            jax.ShapeDtypeStruct((1, 128), jnp.int32),
        ],
        scratch_shapes=[pltpu.VMEM((CLASS_ROWS, t), F32)],
        compiler_params=pltpu.CompilerParams(vmem_limit_bytes=V7X_VMEM_LIMIT),
        name="positions",
    )(cls)


def _invert_kernel(pos_ref, tokg_ref, toks_ref, *, tm):
    t = pos_ref.shape[0]
    ng = tokg_ref.shape[0]
    assert (2 * tm) & (2 * tm - 1) == 0 and toks_ref.shape[0] == ng + tm

    def fill(j, carry):
        for u in range(INVERT_UNROLL):
            k = j * INVERT_UNROLL + u
            tokg_ref[k] = 0
            toks_ref[k] = t + (k & (2 * tm - 1))
        return carry

    lax.fori_loop(0, ng // INVERT_UNROLL, fill, 0)

    def fill_tail(j, carry):
        for u in range(INVERT_UNROLL):
            k = ng + j * INVERT_UNROLL + u
            toks_ref[k] = t + (k & (2 * tm - 1))
        return carry

    lax.fori_loop(0, tm // INVERT_UNROLL, fill_tail, 0)

    def put(j, carry):
        k0 = j * INVERT_UNROLL
        dst = [pos_ref[k0 + u] for u in range(INVERT_UNROLL)]
        for u in range(INVERT_UNROLL):
            tokg_ref[dst[u]] = k0 + u
            toks_ref[2 * tm + dst[u]] = k0 + u
        return carry

    lax.fori_loop(0, t // INVERT_UNROLL, put, 0)


def _invert(pos, nt, tm):
    return pl.pallas_call(
        functools.partial(_invert_kernel, tm=tm),
        in_specs=[pl.BlockSpec(memory_space=pltpu.SMEM)],
        out_specs=[pl.BlockSpec(memory_space=pltpu.SMEM), pl.BlockSpec(memory_space=pltpu.SMEM)],
        out_shape=[jax.ShapeDtypeStruct(((nt + 1) * tm,), jnp.int32),
                   jax.ShapeDtypeStruct(((nt + 2) * tm,), jnp.int32)],
        name="invert_positions",
    )(pos)


def _moe_kernel(tokg_ref, toks_ref, ea_ref, eb_ref, nused_ref, x_hbm, rw_ref, wga_ref, wua_ref,
                wda_ref, wgb_ref, wub_ref, wdb_ref, g_ref, b_ref, out_hbm, gbuf, gsem, obuf, osem,
                *, tm, out_slab):
    i = pl.program_id(0)
    nused = nused_ref[0]
    slot = lax.rem(i, 2)
    other = 1 - slot
    slab_rows = tm * SUBLANES

    def slab_of(buf, sl, r):
        return buf.at[pl.ds(pl.multiple_of((sl * tm + r) * SUBLANES, SUBLANES), SUBLANES), :]

    def start_gather(tile, sl):
        for r in range(tm):
            pltpu.make_async_copy(x_hbm.at[tokg_ref[tile * tm + r]], slab_of(gbuf, sl, r),
                                  gsem.at[sl]).start()

    def wait_gather(sl):
        whole = gbuf.at[pl.ds(pl.multiple_of(sl * slab_rows, SUBLANES), slab_rows), :]
        pltpu.make_async_copy(whole, whole, gsem.at[sl]).wait()

    def start_scatter(tile, sl):
        for r in range(tm):
            tok = toks_ref[(tile + 2) * tm + r]
            if out_slab:
                src, dst = slab_of(obuf, sl, r), out_hbm.at[tok]
            else:
                src, dst = obuf.at[sl, pl.ds(r, 1), :], out_hbm.at[pl.ds(tok, 1), :]
            pltpu.make_async_copy(src, dst, osem.at[sl]).start()

    def wait_scatter(sl):
        if out_slab:
            whole = obuf.at[pl.ds(pl.multiple_of(sl * slab_rows, SUBLANES), slab_rows), :]
        else:
            whole = obuf.at[sl, pl.ds(0, tm), :]
        pltpu.make_async_copy(whole, whole, osem.at[sl]).wait()

    def spare_of(sl):
        if out_slab:
            return obuf.at[pl.ds(pl.multiple_of(2 * slab_rows + sl * SUBLANES, SUBLANES), SUBLANES), :]
        return obuf.at[sl, pl.ds(tm, SUBLANES), pl.ds(0, LANES)]

    @pl.when(i == 0)
    def _():
        obuf[...] = jnp.zeros_like(obuf)
        start_gather(0, 0)
        start_scatter(-2, 0)

    @pl.when(i < nused)
    def _():
        wait_gather(slot)
        x = _slab_load(gbuf, pl.multiple_of(slot * slab_rows, SUBLANES), tm)
        xb = x.astype(BF16)
        start_gather(i + 1, other)
        probe = gbuf[pl.ds(pl.multiple_of(slot * slab_rows, SUBLANES), SUBLANES), :]
        after_gather = _zero_row(probe, wga_ref.shape[1])
        start_scatter(i - 1, other)
        spare_of(slot)[...] = probe
        after_scatter = _zero_row(spare_of(slot)[...], wgb_ref.shape[1])
        scores = jax.nn.sigmoid(jnp.dot(xb, rw_ref[...], preferred_element_type=F32))
        lane = lax.broadcasted_iota(jnp.int32, scores.shape, 1)
        sa = jnp.sum(jnp.where(lane == ea_ref[i], scores, 0.0), axis=-1, keepdims=True)
        sb = jnp.sum(jnp.where(lane == eb_ref[i], scores, 0.0), axis=-1, keepdims=True)
        denom = sa + sb

        def expert(wg_ref, wu_ref, wd_ref, zero_row):
            gt = jnp.dot(xb, wg_ref[...], preferred_element_type=F32)
            up = jnp.dot(xb, wu_ref[...], preferred_element_type=F32)
            hid = (jax.nn.silu(gt) * up + zero_row).astype(BF16)
            return jnp.dot(hid, wd_ref[...], preferred_element_type=F32)

        f = (sa / denom) * expert(wga_ref, wua_ref, wda_ref, after_gather)
        f = f + (sb / denom) * expert(wgb_ref, wub_ref, wdb_ref, after_scatter)
        y = _layer_norm(DEEPNORM_ALPHA * x + f, g_ref[...], b_ref[...])
        wait_scatter(slot)
        if out_slab:
            _slab_store(obuf, pl.multiple_of(slot * slab_rows, SUBLANES), y)
        else:
            obuf[slot, pl.ds(0, tm), :] = y

    @pl.when(i == nused)
    def _():
        wait_gather(slot)
        start_scatter(i - 1, other)
        wait_scatter(slot)
        wait_scatter(other)


def _moe(tokg, toks, tile_ea, tile_eb, nused, x1, rw_pad, wg, wu, wd, g, b, layer, tm, out_slab):
    steps = tile_ea.shape[0]
    t = x1.shape[0]
    d = SUBLANES * LANES
    ff = wg.shape[-1]
    const = lambda shape: pl.BlockSpec(shape, lambda i, *_: (0,) * len(shape))
    up_a = pl.BlockSpec((None, None, d, ff), lambda i, tg, ts, ea, *_: (layer, ea[i], 0, 0))
    up_b = pl.BlockSpec((None, None, d, ff), lambda i, tg, ts, ea, eb, *_: (layer, eb[i], 0, 0))
    dn_a = pl.BlockSpec((None, None, ff, d), lambda i, tg, ts, ea, *_: (layer, ea[i], 0, 0))
    dn_b = pl.BlockSpec((None, None, ff, d), lambda i, tg, ts, ea, eb, *_: (layer, eb[i], 0, 0))
    rows_out = t + 2 * tm
    if out_slab:
        out_shape = jax.ShapeDtypeStruct((rows_out, SUBLANES, LANES), F32)
        obuf = pltpu.VMEM((2 * tm * SUBLANES + 2 * SUBLANES, LANES), F32)
    else:
        out_shape = jax.ShapeDtypeStruct((rows_out, d), F32)
        obuf = pltpu.VMEM((2, tm + SUBLANES, d), F32)
    return pl.pallas_call(
        functools.partial(_moe_kernel, tm=tm, out_slab=out_slab),
        grid_spec=pltpu.PrefetchScalarGridSpec(
            num_scalar_prefetch=5,
            grid=(steps,),
            in_specs=[
                pl.BlockSpec(memory_space=pl.ANY), const((d, 128)),
                up_a, up_a, dn_a, up_b, up_b, dn_b,
                const((1, d)), const((1, d)),
            ],
            out_specs=pl.BlockSpec(memory_space=pl.ANY),
            scratch_shapes=[
                pltpu.VMEM((2 * tm * SUBLANES, LANES), F32), pltpu.SemaphoreType.DMA((2,)),
                obuf, pltpu.SemaphoreType.DMA((2,)),
            ],
        ),
        out_shape=out_shape,
        compiler_params=pltpu.CompilerParams(
            dimension_semantics=("arbitrary",), vmem_limit_bytes=V7X_VMEM_LIMIT),
        name=f"moe_{layer}",
    )(tokg, toks, tile_ea, tile_eb, nused, x1, rw_pad, wg, wu, wd, wg, wu, wd, g, b)


def _route_and_experts(x1_slab, cls, rw_pad, wg, wu, wd, g, b, layer, tm, out_slab):
    t = cls.shape[1]
    nt = t // tm + N_CLASSES
    ntp = -(-(nt + 1) // 128) * 128
    pos, tile_ea, tile_eb, nused = _positions(cls, tm, ntp)
    tokg, toks = _invert(pos.reshape(t), nt, tm)
    per_step = lambda a: a.reshape(ntp)[:nt + 1]
    return _moe(tokg, toks, per_step(tile_ea), per_step(tile_eb), nused.reshape(128)[:1],
                x1_slab.reshape(-1, SUBLANES, LANES), rw_pad, wg, wu, wd, g, b, layer, tm, out_slab)


def _forward(x, a_w_in, a_conv_w, a_w_out, b_w_in, b_ln_g, b_ln_b, b_ws, b_bs, b_w_out,
             router_w, router_bias, moe_w_gate, moe_w_up, moe_w_down,
             ln_mix_g, ln_mix_b, ln_ffn_g, ln_ffn_b, *, ts_conv, ts_sgu, tm):
    bn, s, d = x.shape
    assert d == SUBLANES * LANES
    t = bn * s
    rwt = jnp.concatenate(_split_bf16(router_w.T.astype(F32)), axis=0)
    rb = router_bias.astype(F32).reshape(N_EXPERTS, 1)
    rw_pad = jnp.pad(router_w, ((0, 0), (0, 128 - N_EXPERTS))).astype(BF16)
    wg = moe_w_gate.astype(BF16)
    wu = moe_w_up.astype(BF16)
    wd = moe_w_down.astype(BF16)
    row = lambda a: a.reshape(1, -1)

    x1, cls = _conv_mixer(x, a_w_in[0].astype(BF16), a_conv_w[0], a_w_out[0].astype(BF16),
                          row(ln_mix_g[0]), row(ln_mix_b[0]), rwt, rb, ts_conv)
    x2 = _route_and_experts(x1, cls, rw_pad, wg, wu, wd,
                            row(ln_ffn_g[0]), row(ln_ffn_b[0]), 0, tm, True)
    x3, cls = _sgu_mixer(x2.reshape(-1, LANES), t, b_w_in[0].astype(BF16),
                         row(b_ln_g[0]), row(b_ln_b[0]), b_ws[0], b_bs[0].T,
                         b_w_out[0].astype(BF16), row(ln_mix_g[1]), row(ln_mix_b[1]), rwt, rb,
                         ts_sgu)
    x4 = _route_and_experts(x3, cls, rw_pad, wg, wu, wd,
                            row(ln_ffn_g[1]), row(ln_ffn_b[1]), 1, tm, False)
    return x4[:t].reshape(bn, s, d)


def kernel(x, a_w_in, a_conv_w, a_w_out, b_w_in, b_ln_g, b_ln_b, b_ws, b_bs, b_w_out, router_w, router_bias, moe_w_gate, moe_w_up, moe_w_down, ln_mix_g, ln_mix_b, ln_ffn_g, ln_ffn_b):
    return _forward(x, a_w_in, a_conv_w, a_w_out, b_w_in, b_ln_g, b_ln_b, b_ws, b_bs, b_w_out,
                    router_w, router_bias, moe_w_gate, moe_w_up, moe_w_down,
                    ln_mix_g, ln_mix_b, ln_ffn_g, ln_ffn_b,
                    ts_conv=512, ts_sgu=256, tm=256)
```

```python
import functools

import jax
import jax.numpy as jnp
from jax import lax
from jax.experimental import pallas as pl
from jax.experimental.pallas import tpu as pltpu

F32 = jnp.float32
BF16 = jnp.bfloat16

N_EXPERTS = 16
N_GROUPS = 4
EXPERTS_PER_GROUP = 4
CHUNK = 128
SGU_HEADS = 8
CONV_WIDTH = 3
DEPTH = 2
DEEPNORM_ALPHA = (2 * DEPTH) ** 0.25
LN_EPS = 1e-5

_PAIRS = ((0, 1), (0, 2), (0, 3), (1, 3), (2, 3), (2, 1))
N_PAIRS = len(_PAIRS)
N_CLASSES = N_GROUPS * N_PAIRS
CLASS_ROWS = 32
_CLASS_EA = tuple(g * EXPERTS_PER_GROUP + p[0] for g in range(N_GROUPS) for p in _PAIRS)
_CLASS_EB = tuple(g * EXPERTS_PER_GROUP + p[1] for g in range(N_GROUPS) for p in _PAIRS)

LANES = 128
SUBLANES = 8
V7X_VMEM_LIMIT = 56 * 1024 * 1024
ROW_UNROLL = 16
INVERT_UNROLL = 16


def _layer_norm(y, g, b):
    mu = jnp.mean(y, axis=-1, keepdims=True)
    d = y - mu
    var = jnp.mean(d * d, axis=-1, keepdims=True)
    return d * lax.rsqrt(var + LN_EPS) * g + b


def _gelu(x):
    return 0.5 * x * (1.0 + lax.erf(x * (2.0 ** -0.5)))


def _slab_load(ref, base, rows):
    return jnp.concatenate(
        [ref[pl.ds(base + s, rows, stride=SUBLANES), :] for s in range(SUBLANES)], axis=1)


def _slab_store(ref, base, val):
    rows = val.shape[0]
    for s in range(SUBLANES):
        ref[pl.ds(base + s, rows, stride=SUBLANES), :] = val[:, s * LANES:(s + 1) * LANES]


def _split_bf16(a):
    hi = a.astype(BF16)
    return hi, (a - hi.astype(F32)).astype(BF16)


def _route_class(x1, rw_split, rb):
    x_hi, x_lo = _split_bf16(x1)
    nt_dims = (((1,), (1,)), ((), ()))
    by_hi = lax.dot_general(rw_split, x_hi, nt_dims, preferred_element_type=F32)
    by_lo = lax.dot_general(rw_split[:N_EXPERTS, :], x_lo, nt_dims, preferred_element_type=F32)
    logits = by_hi[:N_EXPERTS, :] + (by_hi[N_EXPERTS:, :] + by_lo)
    biased = jax.nn.sigmoid(logits) + rb
    v = [biased[e:e + 1, :] for e in range(N_EXPERTS)]
    best_score = None
    best_cls = None
    for g in range(N_GROUPS):
        vg = v[g * EXPERTS_PER_GROUP:(g + 1) * EXPERTS_PER_GROUP]
        sel = []
        for i in range(EXPERTS_PER_GROUP):
            rank = jnp.zeros_like(vg[i], dtype=jnp.int32)
            for j in range(EXPERTS_PER_GROUP):
                if j == i:
                    continue
                beats = (vg[j] >= vg[i]) if j < i else (vg[j] > vg[i])
                rank = rank + beats.astype(jnp.int32)
            sel.append(rank < 2)
        top_sum = None
        cls_g = jnp.zeros_like(vg[0], dtype=jnp.int32)
        for p, (a, b) in enumerate(_PAIRS):
            is_pair = sel[a] & sel[b]
            cls_g = jnp.where(is_pair, g * N_PAIRS + p, cls_g)
            pair_sum = vg[min(a, b)] + vg[max(a, b)]
            top_sum = jnp.where(is_pair, pair_sum, 0.0 if top_sum is None else top_sum)
        if best_score is None:
            best_score, best_cls = top_sum, cls_g
        else:
            better = top_sum > best_score
            best_score = jnp.where(better, top_sum, best_score)
            best_cls = jnp.where(better, cls_g, best_cls)
    return best_cls


def _finish_rows(pre_ref, g_ref, b_ref, rwt_ref, rb_ref, x1_ref, cls_ref, r0, nrows):
    x1 = _layer_norm(pre_ref[r0:r0 + nrows, :], g_ref[...], b_ref[...])
    _slab_store(x1_ref, r0 * SUBLANES, x1)
    cls_ref[:, r0:r0 + nrows] = _route_class(x1, rwt_ref[...], rb_ref[...])


def _conv_mixer_kernel(x_ref, w_in_ref, cw_ref, w_out_ref, g_ref, b_ref, rwt_ref, rb_ref,
                       x1_ref, cls_ref, carry_ref, pre_ref, *, tiles_per_seq):
    ts, d = x_ref.shape
    s = pl.program_id(0)

    @pl.when(s == 0)
    def _():
        pre_ref[...] = jnp.zeros_like(pre_ref)

    @pl.when(lax.rem(s, tiles_per_seq) == 0)
    def _():
        carry_ref[...] = jnp.zeros_like(carry_ref)

    x = x_ref[...]
    xb = x.astype(BF16)
    n_chunks = 2
    piece = 2 * d // n_chunks
    rows_per = ts // n_chunks
    parts = []
    for k in range(n_chunks):
        parts.append(jnp.dot(xb, w_in_ref[:, d + k * piece:d + (k + 1) * piece],
                             preferred_element_type=F32))
        _finish_rows(pre_ref, g_ref, b_ref, rwt_ref, rb_ref, x1_ref, cls_ref,
                     k * rows_per, rows_per)
    gate_h = jnp.concatenate(parts, axis=1)
    ch = gate_h[:, 0:d] * gate_h[:, d:2 * d]
    prev2 = carry_ref[6:7, :]
    prev1 = carry_ref[7:8, :]
    rows = lax.broadcasted_iota(jnp.int32, (ts, 1), 0)
    ch1 = jnp.where(rows == 0, prev1, pltpu.roll(ch, 1, 0))
    ch2 = jnp.where(rows == 0, prev2, jnp.where(rows == 1, prev1, pltpu.roll(ch, 2, 0)))
    z = cw_ref[0:1, :] * ch2 + cw_ref[1:2, :] * ch1 + cw_ref[2:3, :] * ch
    carry_ref[...] = ch[ts - 8:ts, :]
    bg = jnp.dot(xb, w_in_ref[:, 0:d], preferred_element_type=F32)
    m = jnp.dot((bg * z).astype(BF16), w_out_ref[...], preferred_element_type=F32)
    pre_ref[...] = DEEPNORM_ALPHA * x + m


def _conv_mixer(x, w_in, conv_w, w_out, ln_g, ln_b, rwt, rb, ts):
    bn, s, d = x.shape
    ns = s // ts
    n = bn * ns
    const = lambda shape: pl.BlockSpec(shape, lambda i: (0,) * len(shape))
    cur = lambda i: jnp.minimum(i, n - 1)
    prev = lambda i: jnp.maximum(i - 1, 0)
    return pl.pallas_call(
        functools.partial(_conv_mixer_kernel, tiles_per_seq=ns),
        grid=(n + 1,),
        in_specs=[
            pl.BlockSpec((None, ts, d), lambda i: (cur(i) // ns, cur(i) % ns, 0)),
            const((d, 3 * d)), const((CONV_WIDTH, d)), const((d, d)),
            const((1, d)), const((1, d)), const((2 * N_EXPERTS, d)), const((N_EXPERTS, 1)),
        ],
        out_specs=[
            pl.BlockSpec((ts * SUBLANES, LANES), lambda i: (prev(i), 0)),
            pl.BlockSpec((1, ts), lambda i: (0, prev(i))),
        ],
        out_shape=[
            jax.ShapeDtypeStruct((bn * s * SUBLANES, LANES), F32),
            jax.ShapeDtypeStruct((1, bn * s), jnp.int32),
        ],
        scratch_shapes=[pltpu.VMEM((8, d), F32), pltpu.VMEM((ts, d), F32)],
        compiler_params=pltpu.CompilerParams(
            dimension_semantics=("arbitrary",), vmem_limit_bytes=V7X_VMEM_LIMIT),
        name="conv_mixer",
    )(x, w_in, conv_w, w_out, ln_g, ln_b, rwt, rb)


def _sgu_mixer_kernel(xs_ref, w_in_ref, lng_ref, lnb_ref, ws_ref, bst_ref, w_out_ref,
                      g_ref, b_ref, rwt_ref, rb_ref, x1_ref, cls_ref, v_ref, gate_ref):
    ts, width = v_ref.shape
    hd = width // SGU_HEADS
    x = _slab_load(xs_ref, 0, ts)
    xb = x.astype(BF16)
    v = _gelu(jnp.dot(xb, w_in_ref[:, width:2 * width], preferred_element_type=F32))
    v_ref[...] = _layer_norm(v, lng_ref[...], lnb_ref[...]).astype(BF16)
    r_i = lax.broadcasted_iota(jnp.int32, (CHUNK, CHUNK), 0)
    c_i = lax.broadcasted_iota(jnp.int32, (CHUNK, CHUNK), 1)
    causal = r_i >= c_i
    for h in range(SGU_HEADS):
        cols = slice(h * hd, (h + 1) * hd)
        u_h = _gelu(jnp.dot(xb, w_in_ref[:, cols], preferred_element_type=F32))
        w_h = jnp.where(causal, ws_ref[h], 0.0).astype(BF16)
        bias = bst_ref[:, h:h + 1]
        for c in range(ts // CHUNK):
            rws = slice(c * CHUNK, (c + 1) * CHUNK)
            mixed = jnp.dot(w_h, v_ref[rws, cols], preferred_element_type=F32) + bias
            gate_ref[rws, cols] = (u_h[rws, :] * mixed).astype(BF16)
    m = jnp.dot(gate_ref[...], w_out_ref[...], preferred_element_type=F32)
    x1 = _layer_norm(DEEPNORM_ALPHA * x + m, g_ref[...], b_ref[...])
    _slab_store(x1_ref, 0, x1)
    cls_ref[...] = _route_class(x1, rwt_ref[...], rb_ref[...])


def _sgu_mixer(xs, w_in, ln_g, ln_b, ws, bst, w_out, g, b, rwt, rb, ts):
    t = xs.shape[0] // SUBLANES
    width, d = w_out.shape
    const = lambda shape: pl.BlockSpec(shape, lambda i: (0,) * len(shape))
    return pl.pallas_call(
        _sgu_mixer_kernel,
        grid=(t // ts,),
        in_specs=[
            pl.BlockSpec((ts * SUBLANES, LANES), lambda i: (i, 0)),
            const((d, 2 * width)), const((1, width)), const((1, width)),
            const((SGU_HEADS, CHUNK, CHUNK)), const((CHUNK, SGU_HEADS)), const((width, d)),
            const((1, d)), const((1, d)), const((2 * N_EXPERTS, d)), const((N_EXPERTS, 1)),
        ],
        out_specs=[
            pl.BlockSpec((ts * SUBLANES, LANES), lambda i: (i, 0)),
            pl.BlockSpec((1, ts), lambda i: (0, i)),
        ],
        out_shape=[
            jax.ShapeDtypeStruct((t * SUBLANES, LANES), F32),
            jax.ShapeDtypeStruct((1, t), jnp.int32),
        ],
        scratch_shapes=[pltpu.VMEM((ts, width), BF16), pltpu.VMEM((ts, width), BF16)],
        compiler_params=pltpu.CompilerParams(
            dimension_semantics=("arbitrary",), vmem_limit_bytes=V7X_VMEM_LIMIT),
        name="sgu_mixer",
    )(xs, w_in, ln_g, ln_b, ws, bst, w_out, g, b, rwt, rb)


def _positions_kernel(cls_ref, pos_ref, tile_ea_ref, tile_eb_ref, nvalid_ref, nused_ref,
                      incl_ref, *, tm):
    t = cls_ref.shape[1]
    ntp = tile_ea_ref.shape[1]
    lanes = 256
    cls = cls_ref[...]
    crow = lax.broadcasted_iota(jnp.int32, (CLASS_ROWS, t), 0)
    onehot = (crow == cls).astype(F32)
    k_i = lax.broadcasted_iota(jnp.int32, (lanes, lanes), 0)
    j_i = lax.broadcasted_iota(jnp.int32, (lanes, lanes), 1)
    upper = (k_i <= j_i).astype(BF16)
    count = jnp.zeros((CLASS_ROWS, 1), F32)
    for c in range(t // lanes):
        sl = slice(c * lanes, (c + 1) * lanes)
        inc = jnp.dot(onehot[:, sl].astype(BF16), upper, preferred_element_type=F32) + count
        incl_ref[:, sl] = inc
        count = inc[:, lanes - 1:lanes]
    rank = jnp.sum(onehot * incl_ref[...], axis=0, keepdims=True) - 1.0
    ntile = jnp.floor((count + (tm - 1)) * (1.0 / tm))
    srow = lax.broadcasted_iota(jnp.int32, (CLASS_ROWS, 1), 0)
    tstart = jnp.zeros((CLASS_ROWS, 1), F32)
    for c in range(N_CLASSES):
        tstart = tstart + jnp.where(srow > c, ntile[c:c + 1, :], 0.0)
    tend = tstart + ntile
    pos = jnp.sum(onehot * (tstart * tm), axis=0, keepdims=True) + rank
    pos_ref[...] = pos.astype(jnp.int32)
    nused = jnp.max(jnp.where(srow < N_CLASSES, tend, 0.0), axis=0, keepdims=True)
    last_cls = jnp.max(jnp.where((ntile > 0) & (srow < N_CLASSES), srow, 0), axis=0, keepdims=True)
    tile_i = lax.broadcasted_iota(jnp.int32, (CLASS_ROWS, ntp), 1).astype(F32)
    is_cls = lax.broadcasted_iota(jnp.int32, (CLASS_ROWS, ntp), 0) < N_CLASSES
    done = (tile_i >= tend) & is_cls
    tcls = jnp.minimum(jnp.sum(done.astype(jnp.int32), axis=0, keepdims=True), last_cls)
    ea = jnp.zeros((1, ntp), jnp.int32)
    eb = jnp.zeros((1, ntp), jnp.int32)
    for c in range(N_CLASSES):
        ea = jnp.where(tcls == c, _CLASS_EA[c], ea)
        eb = jnp.where(tcls == c, _CLASS_EB[c], eb)
    tile_ea_ref[...] = ea
    tile_eb_ref[...] = eb
    inside = (tile_i >= tstart) & (tile_i < tend) & is_cls
    left = jnp.minimum(count - (tile_i - tstart) * tm, float(tm))
    nvalid_ref[...] = jnp.sum(jnp.where(inside, left, 0.0), axis=0, keepdims=True).astype(jnp.int32)
    nused_ref[...] = jnp.broadcast_to(nused, nused_ref.shape).astype(jnp.int32)


def _positions(cls, tm, ntp):
    t = cls.shape[1]
    return pl.pallas_call(
        functools.partial(_positions_kernel, tm=tm),
        out_shape=[
            jax.ShapeDtypeStruct((1, t), jnp.int32),
            jax.ShapeDtypeStruct((1, ntp), jnp.int32),
            jax.ShapeDtypeStruct((1, ntp), jnp.int32),
            jax.ShapeDtypeStruct((1, ntp), jnp.int32),
            jax.ShapeDtypeStruct((1, 128), jnp.int32),
        ],
        scratch_shapes=[pltpu.VMEM((CLASS_ROWS, t), F32)],
        compiler_params=pltpu.CompilerParams(vmem_limit_bytes=V7X_VMEM_LIMIT),
        name="positions",
    )(cls)


def _invert_kernel(pos_ref, tok_ref):
    t = pos_ref.shape[0]
    r = tok_ref.shape[0]

    def fill(j, carry):
        for u in range(INVERT_UNROLL):
            tok_ref[j * INVERT_UNROLL + u] = 0
        return carry

    lax.fori_loop(0, r // INVERT_UNROLL, fill, 0)

    def put(j, carry):
        k0 = j * INVERT_UNROLL
        dst = [pos_ref[k0 + u] for u in range(INVERT_UNROLL)]
        for u in range(INVERT_UNROLL):
            tok_ref[dst[u]] = k0 + u
        return carry

    lax.fori_loop(0, t // INVERT_UNROLL, put, 0)


def _invert(pos, r):
    return pl.pallas_call(
        _invert_kernel,
        in_specs=[pl.BlockSpec(memory_space=pltpu.SMEM)],
        out_specs=pl.BlockSpec(memory_space=pltpu.SMEM),
        out_shape=jax.ShapeDtypeStruct((r,), jnp.int32),
        name="invert_positions",
    )(pos)


def _for_rows(n, per_row, per_group=None):
    ngroups = n // ROW_UNROLL

    def group(j, carry):
        if per_group is not None:
            per_group()
        else:
            for u in range(ROW_UNROLL):
                per_row(j * ROW_UNROLL + u)
        return carry

    lax.fori_loop(0, ngroups, group, 0)

    def single(r, carry):
        per_row(r)
        return carry

    lax.fori_loop(ngroups * ROW_UNROLL, n, single, 0)


def _moe_kernel(tok_ref, ea_ref, eb_ref, nvalid_ref, nused_ref, x_hbm, rw_ref, wga_ref, wua_ref,
                wda_ref, wgb_ref, wub_ref, wdb_ref, g_ref, b_ref, out_hbm, gbuf, gsem, obuf, osem,
                pre_ref, *, tm, out_slab):
    i = pl.program_id(0)
    nused = nused_ref[0]
    slot = lax.rem(i, 2)
    other = 1 - slot
    slab_rows = tm * SUBLANES

    def gather_copy(tile, sl, r):
        dst = pl.multiple_of((sl * tm + r) * SUBLANES, SUBLANES)
        return pltpu.make_async_copy(x_hbm.at[tok_ref[tile * tm + r]],
                                     gbuf.at[pl.ds(dst, SUBLANES), :], gsem.at[sl])

    def start_gather(tile, sl):
        _for_rows(nvalid_ref[tile], lambda r: gather_copy(tile, sl, r).start())

    def wait_gather(tile, sl):
        group = pltpu.make_async_copy(gbuf.at[pl.ds(0, ROW_UNROLL * SUBLANES), :],
                                      gbuf.at[pl.ds(0, ROW_UNROLL * SUBLANES), :], gsem.at[sl])
        _for_rows(nvalid_ref[tile], lambda r: gather_copy(tile, sl, r).wait(), group.wait)

    def scatter_copy(tile, sl, r):
        tok = tok_ref[tile * tm + r]
        if out_slab:
            src = pl.multiple_of((sl * tm + r) * SUBLANES, SUBLANES)
            return pltpu.make_async_copy(obuf.at[pl.ds(src, SUBLANES), :], out_hbm.at[tok],
                                         osem.at[sl])
        return pltpu.make_async_copy(obuf.at[sl, pl.ds(r, 1), :], out_hbm.at[pl.ds(tok, 1), :],
                                     osem.at[sl])

    def start_scatter(tile, sl):
        _for_rows(nvalid_ref[tile], lambda r: scatter_copy(tile, sl, r).start())

    def wait_scatter(tile, sl):
        if out_slab:
            part = obuf.at[pl.ds(0, ROW_UNROLL * SUBLANES), :]
        else:
            part = obuf.at[sl, pl.ds(0, ROW_UNROLL), :]
        group = pltpu.make_async_copy(part, part, osem.at[sl])
        _for_rows(nvalid_ref[tile], lambda r: scatter_copy(tile, sl, r).wait(), group.wait)

    def finish_rows(r0, nrows):
        y = _layer_norm(pre_ref[r0:r0 + nrows, :], g_ref[...], b_ref[...])
        if out_slab:
            _slab_store(obuf, pl.multiple_of(other * slab_rows + r0 * SUBLANES, SUBLANES), y)
        else:
            obuf[other, r0:r0 + nrows, :] = y

    @pl.when(i == 0)
    def _():
        gbuf[...] = jnp.zeros_like(gbuf)
        pre_ref[...] = jnp.zeros_like(pre_ref)

    @pl.when((i == 0) & (nused > 0))
    def _():
        start_gather(0, 0)

    @pl.when(i + 1 < nused)
    def _():
        start_gather(i + 1, other)

    @pl.when((i >= 3) & (i <= nused))
    def _():
        wait_scatter(i - 3, other)

    @pl.when(i < nused)
    def _():
        wait_gather(i, slot)
        x = _slab_load(gbuf, pl.multiple_of(slot * slab_rows, SUBLANES), tm)
        xb = x.astype(BF16)
        half = tm // 2
        scores = jax.nn.sigmoid(jnp.dot(xb, rw_ref[...], preferred_element_type=F32))
        lane = lax.broadcasted_iota(jnp.int32, scores.shape, 1)
        sa = jnp.sum(jnp.where(lane == ea_ref[i], scores, 0.0), axis=-1, keepdims=True)
        sb = jnp.sum(jnp.where(lane == eb_ref[i], scores, 0.0), axis=-1, keepdims=True)
        denom = sa + sb
        gt = jnp.dot(xb, wga_ref[...], preferred_element_type=F32)
        finish_rows(0, half)
        up = jnp.dot(xb, wua_ref[...], preferred_element_type=F32)
        finish_rows(half, half)
        hid = (jax.nn.silu(gt) * up).astype(BF16)
        f = (sa / denom) * jnp.dot(hid, wda_ref[...], preferred_element_type=F32)
        gt = jnp.dot(xb, wgb_ref[...], preferred_element_type=F32)
        up = jnp.dot(xb, wub_ref[...], preferred_element_type=F32)
        hid = (jax.nn.silu(gt) * up).astype(BF16)
        f = f + (sb / denom) * jnp.dot(hid, wdb_ref[...], preferred_element_type=F32)
        pre_ref[...] = DEEPNORM_ALPHA * x + f

    @pl.when(i == nused)
    def _():
        finish_rows(0, tm)

    @pl.when((i >= 1) & (i <= nused))
    def _():
        start_scatter(i - 1, other)

    @pl.when(i == nused)
    def _():
        for back in (2, 1):
            tile = i - back

            @pl.when(tile >= 0)
            def _():
                wait_scatter(tile, lax.rem(tile, 2))


def _moe(tok, tile_ea, tile_eb, nvalid, nused, x1, rw_pad, wg, wu, wd, g, b, layer, tm, out_slab):
    steps = tile_ea.shape[0]
    t = x1.shape[0]
    d = SUBLANES * LANES
    ff = wg.shape[-1]
    const = lambda shape: pl.BlockSpec(shape, lambda i, *_: (0,) * len(shape))
    up_a = pl.BlockSpec((None, None, d, ff), lambda i, tok, ea, *_: (layer, ea[i], 0, 0))
    up_b = pl.BlockSpec((None, None, d, ff), lambda i, tok, ea, eb, *_: (layer, eb[i], 0, 0))
    dn_a = pl.BlockSpec((None, None, ff, d), lambda i, tok, ea, *_: (layer, ea[i], 0, 0))
    dn_b = pl.BlockSpec((None, None, ff, d), lambda i, tok, ea, eb, *_: (layer, eb[i], 0, 0))
    if out_slab:
        out_shape = jax.ShapeDtypeStruct((t, SUBLANES, LANES), F32)
        obuf = pltpu.VMEM((2 * tm * SUBLANES, LANES), F32)
    else:
        out_shape = jax.ShapeDtypeStruct((t, d), F32)
        obuf = pltpu.VMEM((2, tm, d), F32)
    return pl.pallas_call(
        functools.partial(_moe_kernel, tm=tm, out_slab=out_slab),
        grid_spec=pltpu.PrefetchScalarGridSpec(
            num_scalar_prefetch=5,
            grid=(steps,),
            in_specs=[
                pl.BlockSpec(memory_space=pl.ANY), const((d, 128)),
                up_a, up_a, dn_a, up_b, up_b, dn_b,
                const((1, d)), const((1, d)),
            ],
            out_specs=pl.BlockSpec(memory_space=pl.ANY),
            scratch_shapes=[
                pltpu.VMEM((2 * tm * SUBLANES, LANES), F32), pltpu.SemaphoreType.DMA((2,)),
                obuf, pltpu.SemaphoreType.DMA((2,)),
                pltpu.VMEM((tm, d), F32),
            ],
        ),
        out_shape=out_shape,
        compiler_params=pltpu.CompilerParams(
            dimension_semantics=("arbitrary",), vmem_limit_bytes=V7X_VMEM_LIMIT),
        name=f"moe_{layer}",
    )(tok, tile_ea, tile_eb, nvalid, nused, x1, rw_pad, wg, wu, wd, wg, wu, wd, g, b)


def _route_and_experts(x1_slab, cls, rw_pad, wg, wu, wd, g, b, layer, tm, out_slab):
    t = cls.shape[1]
    nt = t // tm + N_CLASSES
    ntp = -(-(nt + 1) // 128) * 128
    pos, tile_ea, tile_eb, nvalid, nused = _positions(cls, tm, ntp)
    tok = _invert(pos.reshape(t), nt * tm)
    per_step = lambda a: a.reshape(ntp)[:nt + 1]
    return _moe(tok, per_step(tile_ea), per_step(tile_eb), per_step(nvalid), nused.reshape(128)[:1],
                x1_slab.reshape(t, SUBLANES, LANES), rw_pad, wg, wu, wd, g, b, layer, tm, out_slab)


def _forward(x, a_w_in, a_conv_w, a_w_out, b_w_in, b_ln_g, b_ln_b, b_ws, b_bs, b_w_out,
             router_w, router_bias, moe_w_gate, moe_w_up, moe_w_down,
             ln_mix_g, ln_mix_b, ln_ffn_g, ln_ffn_b, *, ts_conv, ts_sgu, tm):
    bn, s, d = x.shape
    assert d == SUBLANES * LANES
    t = bn * s
    rwt = jnp.concatenate(_split_bf16(router_w.T.astype(F32)), axis=0)
    rb = router_bias.astype(F32).reshape(N_EXPERTS, 1)
    rw_pad = jnp.pad(router_w, ((0, 0), (0, 128 - N_EXPERTS))).astype(BF16)
    wg = moe_w_gate.astype(BF16)
    wu = moe_w_up.astype(BF16)
    wd = moe_w_down.astype(BF16)
    row = lambda a: a.reshape(1, -1)

    x1, cls = _conv_mixer(x, a_w_in[0].astype(BF16), a_conv_w[0], a_w_out[0].astype(BF16),
                          row(ln_mix_g[0]), row(ln_mix_b[0]), rwt, rb, ts_conv)
    x2 = _route_and_experts(x1, cls, rw_pad, wg, wu, wd,
                            row(ln_ffn_g[0]), row(ln_ffn_b[0]), 0, tm, True)
    x3, cls = _sgu_mixer(x2.reshape(t * SUBLANES, LANES), b_w_in[0].astype(BF16),
                         row(b_ln_g[0]), row(b_ln_b[0]), b_ws[0], b_bs[0].T,
                         b_w_out[0].astype(BF16), row(ln_mix_g[1]), row(ln_mix_b[1]), rwt, rb,
                         ts_sgu)
    x4 = _route_and_experts(x3, cls, rw_pad, wg, wu, wd,
                            row(ln_ffn_g[1]), row(ln_ffn_b[1]), 1, tm, False)
    return x4.reshape(bn, s, d)


def kernel(x, a_w_in, a_conv_w, a_w_out, b_w_in, b_ln_g, b_ln_b, b_ws, b_bs, b_w_out, router_w, router_bias, moe_w_gate, moe_w_up, moe_w_down, ln_mix_g, ln_mix_b, ln_ffn_g, ln_ffn_b):
    return _forward(x, a_w_in, a_conv_w, a_w_out, b_w_in, b_ln_g, b_ln_b, b_ws, b_bs, b_w_out,
                    router_w, router_bias, moe_w_gate, moe_w_up, moe_w_down,
                    ln_mix_g, ln_mix_b, ln_ffn_g, ln_ffn_b,
                    ts_conv=512, ts_sgu=256, tm=256)
```

```python
import functools

import jax
import jax.numpy as jnp
from jax import lax
from jax.experimental import pallas as pl
from jax.experimental.pallas import tpu as pltpu

F32 = jnp.float32
BF16 = jnp.bfloat16

N_EXPERTS = 16
N_GROUPS = 4
EXPERTS_PER_GROUP = 4
CHUNK = 128
SGU_HEADS = 8
CONV_WIDTH = 3
DEPTH = 2
DEEPNORM_ALPHA = (2 * DEPTH) ** 0.25
LN_EPS = 1e-5

_PAIRS = ((0, 1), (0, 2), (0, 3), (1, 3), (2, 3), (2, 1))
N_PAIRS = len(_PAIRS)
N_CLASSES = N_GROUPS * N_PAIRS
CLASS_ROWS = 32
_CLASS_EA = tuple(g * EXPERTS_PER_GROUP + p[0] for g in range(N_GROUPS) for p in _PAIRS)
_CLASS_EB = tuple(g * EXPERTS_PER_GROUP + p[1] for g in range(N_GROUPS) for p in _PAIRS)

LANES = 128
SUBLANES = 8
V7X_VMEM_LIMIT = 56 * 1024 * 1024
ROW_UNROLL = 16
INVERT_UNROLL = 16


def _layer_norm(y, g, b):
    mu = jnp.mean(y, axis=-1, keepdims=True)
    d = y - mu
    var = jnp.mean(d * d, axis=-1, keepdims=True)
    return d * lax.rsqrt(var + LN_EPS) * g + b


def _gelu(x):
    return 0.5 * x * (1.0 + lax.erf(x * (2.0 ** -0.5)))


def _slab_load(ref, base, rows):
    return jnp.concatenate(
        [ref[pl.ds(base + s, rows, stride=SUBLANES), :] for s in range(SUBLANES)], axis=1)


def _slab_store(ref, base, val):
    rows = val.shape[0]
    for s in range(SUBLANES):
        ref[pl.ds(base + s, rows, stride=SUBLANES), :] = val[:, s * LANES:(s + 1) * LANES]


def _split_bf16(a):
    hi = a.astype(BF16)
    return hi, (a - hi.astype(F32)).astype(BF16)


def _route_class(x1, rw_split, rb):
    x_hi, x_lo = _split_bf16(x1)
    nt_dims = (((1,), (1,)), ((), ()))
    by_hi = lax.dot_general(rw_split, x_hi, nt_dims, preferred_element_type=F32)
    by_lo = lax.dot_general(rw_split[:N_EXPERTS, :], x_lo, nt_dims, preferred_element_type=F32)
    logits = by_hi[:N_EXPERTS, :] + (by_hi[N_EXPERTS:, :] + by_lo)
    biased = jax.nn.sigmoid(logits) + rb
    v = [biased[e:e + 1, :] for e in range(N_EXPERTS)]
    best_score = None
    best_cls = None
    for g in range(N_GROUPS):
        vg = v[g * EXPERTS_PER_GROUP:(g + 1) * EXPERTS_PER_GROUP]
        sel = []
        for i in range(EXPERTS_PER_GROUP):
            rank = jnp.zeros_like(vg[i], dtype=jnp.int32)
            for j in range(EXPERTS_PER_GROUP):
                if j == i:
                    continue
                beats = (vg[j] >= vg[i]) if j < i else (vg[j] > vg[i])
                rank = rank + beats.astype(jnp.int32)
            sel.append(rank < 2)
        top_sum = None
        cls_g = jnp.zeros_like(vg[0], dtype=jnp.int32)
        for p, (a, b) in enumerate(_PAIRS):
            is_pair = sel[a] & sel[b]
            cls_g = jnp.where(is_pair, g * N_PAIRS + p, cls_g)
            pair_sum = vg[min(a, b)] + vg[max(a, b)]
            top_sum = jnp.where(is_pair, pair_sum, 0.0 if top_sum is None else top_sum)
        if best_score is None:
            best_score, best_cls = top_sum, cls_g
        else:
            better = top_sum > best_score
            best_score = jnp.where(better, top_sum, best_score)
            best_cls = jnp.where(better, cls_g, best_cls)
    return best_cls


def _finish_rows(pre_ref, g_ref, b_ref, rwt_ref, rb_ref, x1_ref, cls_ref, r0, nrows):
    x1 = _layer_norm(pre_ref[r0:r0 + nrows, :], g_ref[...], b_ref[...])
    _slab_store(x1_ref, r0 * SUBLANES, x1)
    cls_ref[:, r0:r0 + nrows] = _route_class(x1, rwt_ref[...], rb_ref[...])


def _conv_mixer_kernel(x_ref, w_in_ref, cw_ref, w_out_ref, g_ref, b_ref, rwt_ref, rb_ref,
                       x1_ref, cls_ref, carry_ref, pre_ref, *, tiles_per_seq):
    ts, d = x_ref.shape
    s = pl.program_id(0)

    @pl.when(s == 0)
    def _():
        pre_ref[...] = jnp.zeros_like(pre_ref)

    @pl.when(lax.rem(s, tiles_per_seq) == 0)
    def _():
        carry_ref[...] = jnp.zeros_like(carry_ref)

    x = x_ref[...]
    xb = x.astype(BF16)
    n_chunks = 2
    piece = 2 * d // n_chunks
    rows_per = ts // n_chunks
    parts = []
    for k in range(n_chunks):
        parts.append(jnp.dot(xb, w_in_ref[:, d + k * piece:d + (k + 1) * piece],
                             preferred_element_type=F32))
        _finish_rows(pre_ref, g_ref, b_ref, rwt_ref, rb_ref, x1_ref, cls_ref,
                     k * rows_per, rows_per)
    gate_h = jnp.concatenate(parts, axis=1)
    ch = gate_h[:, 0:d] * gate_h[:, d:2 * d]
    prev2 = carry_ref[6:7, :]
    prev1 = carry_ref[7:8, :]
    rows = lax.broadcasted_iota(jnp.int32, (ts, 1), 0)
    ch1 = jnp.where(rows == 0, prev1, pltpu.roll(ch, 1, 0))
    ch2 = jnp.where(rows == 0, prev2, jnp.where(rows == 1, prev1, pltpu.roll(ch, 2, 0)))
    z = cw_ref[0:1, :] * ch2 + cw_ref[1:2, :] * ch1 + cw_ref[2:3, :] * ch
    carry_ref[...] = ch[ts - 8:ts, :]
    bg = jnp.dot(xb, w_in_ref[:, 0:d], preferred_element_type=F32)
    m = jnp.dot((bg * z).astype(BF16), w_out_ref[...], preferred_element_type=F32)
    pre_ref[...] = DEEPNORM_ALPHA * x + m


def _conv_mixer(x, w_in, conv_w, w_out, ln_g, ln_b, rwt, rb, ts):
    bn, s, d = x.shape
    ns = s // ts
    n = bn * ns
    const = lambda shape: pl.BlockSpec(shape, lambda i: (0,) * len(shape))
    cur = lambda i: jnp.minimum(i, n - 1)
    prev = lambda i: jnp.maximum(i - 1, 0)
    return pl.pallas_call(
        functools.partial(_conv_mixer_kernel, tiles_per_seq=ns),
        grid=(n + 1,),
        in_specs=[
            pl.BlockSpec((None, ts, d), lambda i: (cur(i) // ns, cur(i) % ns, 0)),
            const((d, 3 * d)), const((CONV_WIDTH, d)), const((d, d)),
            const((1, d)), const((1, d)), const((2 * N_EXPERTS, d)), const((N_EXPERTS, 1)),
        ],
        out_specs=[
            pl.BlockSpec((ts * SUBLANES, LANES), lambda i: (prev(i), 0)),
            pl.BlockSpec((1, ts), lambda i: (0, prev(i))),
        ],
        out_shape=[
            jax.ShapeDtypeStruct((bn * s * SUBLANES, LANES), F32),
            jax.ShapeDtypeStruct((1, bn * s), jnp.int32),
        ],
        scratch_shapes=[pltpu.VMEM((8, d), F32), pltpu.VMEM((ts, d), F32)],
        compiler_params=pltpu.CompilerParams(
            dimension_semantics=("arbitrary",), vmem_limit_bytes=V7X_VMEM_LIMIT),
        name="conv_mixer",
    )(x, w_in, conv_w, w_out, ln_g, ln_b, rwt, rb)


def _sgu_mixer_kernel(xs_ref, w_in_ref, lng_ref, lnb_ref, ws_ref, bst_ref, w_out_ref,
                      g_ref, b_ref, rwt_ref, rb_ref, x1_ref, cls_ref, v_ref, gate_ref):
    ts, width = v_ref.shape
    hd = width // SGU_HEADS
    x = _slab_load(xs_ref, 0, ts)
    xb = x.astype(BF16)
    v = _gelu(jnp.dot(xb, w_in_ref[:, width:2 * width], preferred_element_type=F32))
    v_ref[...] = _layer_norm(v, lng_ref[...], lnb_ref[...]).astype(BF16)
    r_i = lax.broadcasted_iota(jnp.int32, (CHUNK, CHUNK), 0)
    c_i = lax.broadcasted_iota(jnp.int32, (CHUNK, CHUNK), 1)
    causal = r_i >= c_i
    for h in range(SGU_HEADS):
        cols = slice(h * hd, (h + 1) * hd)
        u_h = _gelu(jnp.dot(xb, w_in_ref[:, cols], preferred_element_type=F32))
        w_h = jnp.where(causal, ws_ref[h], 0.0).astype(BF16)
        bias = bst_ref[:, h:h + 1]
        for c in range(ts // CHUNK):
            rws = slice(c * CHUNK, (c + 1) * CHUNK)
            mixed = jnp.dot(w_h, v_ref[rws, cols], preferred_element_type=F32) + bias
            gate_ref[rws, cols] = (u_h[rws, :] * mixed).astype(BF16)
    m = jnp.dot(gate_ref[...], w_out_ref[...], preferred_element_type=F32)
    x1 = _layer_norm(DEEPNORM_ALPHA * x + m, g_ref[...], b_ref[...])
    _slab_store(x1_ref, 0, x1)
    cls_ref[...] = _route_class(x1, rwt_ref[...], rb_ref[...])


def _sgu_mixer(xs, w_in, ln_g, ln_b, ws, bst, w_out, g, b, rwt, rb, ts):
    t = xs.shape[0] // SUBLANES
    width, d = w_out.shape
    const = lambda shape: pl.BlockSpec(shape, lambda i: (0,) * len(shape))
    return pl.pallas_call(
        _sgu_mixer_kernel,
        grid=(t // ts,),
        in_specs=[
            pl.BlockSpec((ts * SUBLANES, LANES), lambda i: (i, 0)),
            const((d, 2 * width)), const((1, width)), const((1, width)),
            const((SGU_HEADS, CHUNK, CHUNK)), const((CHUNK, SGU_HEADS)), const((width, d)),
            const((1, d)), const((1, d)), const((2 * N_EXPERTS, d)), const((N_EXPERTS, 1)),
        ],
        out_specs=[
            pl.BlockSpec((ts * SUBLANES, LANES), lambda i: (i, 0)),
            pl.BlockSpec((1, ts), lambda i: (0, i)),
        ],
        out_shape=[
            jax.ShapeDtypeStruct((t * SUBLANES, LANES), F32),
            jax.ShapeDtypeStruct((1, t), jnp.int32),
        ],
        scratch_shapes=[pltpu.VMEM((ts, width), BF16), pltpu.VMEM((ts, width), BF16)],
        compiler_params=pltpu.CompilerParams(
            dimension_semantics=("arbitrary",), vmem_limit_bytes=V7X_VMEM_LIMIT),
        name="sgu_mixer",
    )(xs, w_in, ln_g, ln_b, ws, bst, w_out, g, b, rwt, rb)


def _positions_kernel(cls_ref, pos_ref, tile_ea_ref, tile_eb_ref, nvalid_ref, nused_ref,
                      incl_ref, *, tm):
    t = cls_ref.shape[1]
    ntp = tile_ea_ref.shape[1]
    lanes = 256
    cls = cls_ref[...]
    crow = lax.broadcasted_iota(jnp.int32, (CLASS_ROWS, t), 0)
    onehot = (crow == cls).astype(F32)
    k_i = lax.broadcasted_iota(jnp.int32, (lanes, lanes), 0)
    j_i = lax.broadcasted_iota(jnp.int32, (lanes, lanes), 1)
    upper = (k_i <= j_i).astype(BF16)
    count = jnp.zeros((CLASS_ROWS, 1), F32)
    for c in range(t // lanes):
        sl = slice(c * lanes, (c + 1) * lanes)
        inc = jnp.dot(onehot[:, sl].astype(BF16), upper, preferred_element_type=F32) + count
        incl_ref[:, sl] = inc
        count = inc[:, lanes - 1:lanes]
    rank = jnp.sum(onehot * incl_ref[...], axis=0, keepdims=True) - 1.0
    ntile = jnp.floor((count + (tm - 1)) * (1.0 / tm))
    srow = lax.broadcasted_iota(jnp.int32, (CLASS_ROWS, 1), 0)
    tstart = jnp.zeros((CLASS_ROWS, 1), F32)
    for c in range(N_CLASSES):
        tstart = tstart + jnp.where(srow > c, ntile[c:c + 1, :], 0.0)
    tend = tstart + ntile
    pos = jnp.sum(onehot * (tstart * tm), axis=0, keepdims=True) + rank
    pos_ref[...] = pos.astype(jnp.int32)
    nused = jnp.max(jnp.where(srow < N_CLASSES, tend, 0.0), axis=0, keepdims=True)
    last_cls = jnp.max(jnp.where((ntile > 0) & (srow < N_CLASSES), srow, 0), axis=0, keepdims=True)
    tile_i = lax.broadcasted_iota(jnp.int32, (CLASS_ROWS, ntp), 1).astype(F32)
    is_cls = lax.broadcasted_iota(jnp.int32, (CLASS_ROWS, ntp), 0) < N_CLASSES
    done = (tile_i >= tend) & is_cls
    tcls = jnp.minimum(jnp.sum(done.astype(jnp.int32), axis=0, keepdims=True), last_cls)
    ea = jnp.zeros((1, ntp), jnp.int32)
    eb = jnp.zeros((1, ntp), jnp.int32)
    for c in range(N_CLASSES):
        ea = jnp.where(tcls == c, _CLASS_EA[c], ea)
        eb = jnp.where(tcls == c, _CLASS_EB[c], eb)
    tile_ea_ref[...] = ea
    tile_eb_ref[...] = eb
    inside = (tile_i >= tstart) & (tile_i < tend) & is_cls
    left = jnp.minimum(count - (tile_i - tstart) * tm, float(tm))
    nvalid_ref[...] = jnp.sum(jnp.where(inside, left, 0.0), axis=0, keepdims=True).astype(jnp.int32)
    nused_ref[...] = jnp.broadcast_to(nused, nused_ref.shape).astype(jnp.int32)


def _positions(cls, tm, ntp):
    t = cls.shape[1]
    return pl.pallas_call(
        functools.partial(_positions_kernel, tm=tm),
        out_shape=[
            jax.ShapeDtypeStruct((1, t), jnp.int32),
            jax.ShapeDtypeStruct((1, ntp), jnp.int32),
            jax.ShapeDtypeStruct((1, ntp), jnp.int32),
            jax.ShapeDtypeStruct((1, ntp), jnp.int32),
            jax.ShapeDtypeStruct((1, 128), jnp.int32),
        ],
        scratch_shapes=[pltpu.VMEM((CLASS_ROWS, t), F32)],
        compiler_params=pltpu.CompilerParams(vmem_limit_bytes=V7X_VMEM_LIMIT),
        name="positions",
    )(cls)


def _invert_kernel(pos_ref, tok_ref):
    t = pos_ref.shape[0]
    r = tok_ref.shape[0]

    def fill(j, carry):
        for u in range(INVERT_UNROLL):
            tok_ref[j * INVERT_UNROLL + u] = 0
        return carry

    lax.fori_loop(0, r // INVERT_UNROLL, fill, 0)

    def put(j, carry):
        k0 = j * INVERT_UNROLL
        dst = [pos_ref[k0 + u] for u in range(INVERT_UNROLL)]
        for u in range(INVERT_UNROLL):
            tok_ref[dst[u]] = k0 + u
        return carry

    lax.fori_loop(0, t // INVERT_UNROLL, put, 0)


def _invert(pos, r):
    return pl.pallas_call(
        _invert_kernel,
        in_specs=[pl.BlockSpec(memory_space=pltpu.SMEM)],
        out_specs=pl.BlockSpec(memory_space=pltpu.SMEM),
        out_shape=jax.ShapeDtypeStruct((r,), jnp.int32),
        name="invert_positions",
    )(pos)


def _for_rows(n, per_row, per_group=None):
    ngroups = n // ROW_UNROLL

    def group(j, carry):
        if per_group is not None:
            per_group()
        else:
            for u in range(ROW_UNROLL):
                per_row(j * ROW_UNROLL + u, u)
        return carry

    lax.fori_loop(0, ngroups, group, 0)

    def single(r, carry):
        per_row(r, 0)
        return carry

    lax.fori_loop(ngroups * ROW_UNROLL, n, single, 0)


def _moe_kernel(tok_ref, ea_ref, eb_ref, nvalid_ref, nused_ref, x_hbm, rw_ref, wga_ref, wua_ref,
                wda_ref, wgb_ref, wub_ref, wdb_ref, g_ref, b_ref, out_hbm, gbuf, gsem, obuf, osem,
                pre_ref, *, tm, out_slab):
    i = pl.program_id(0)
    nused = nused_ref[0]
    slot = lax.rem(i, 2)
    other = 1 - slot
    slab_rows = tm * SUBLANES

    def gather_copy(tile, sl, r):
        dst = pl.multiple_of((sl * tm + r) * SUBLANES, SUBLANES)
        return pltpu.make_async_copy(x_hbm.at[tok_ref[tile * tm + r]],
                                     gbuf.at[pl.ds(dst, SUBLANES), :], gsem.at[sl])

    def start_gather(tile, sl):
        _for_rows(nvalid_ref[tile],
                  lambda r, lane: gather_copy(tile, sl, r).start(priority=lane % 2))

    def wait_gather(tile, sl):
        group = pltpu.make_async_copy(gbuf.at[pl.ds(0, ROW_UNROLL * SUBLANES), :],
                                      gbuf.at[pl.ds(0, ROW_UNROLL * SUBLANES), :], gsem.at[sl])
        _for_rows(nvalid_ref[tile], lambda r, lane: gather_copy(tile, sl, r).wait(), group.wait)

    def scatter_copy(tile, sl, r):
        tok = tok_ref[tile * tm + r]
        if out_slab:
            src = pl.multiple_of((sl * tm + r) * SUBLANES, SUBLANES)
            return pltpu.make_async_copy(obuf.at[pl.ds(src, SUBLANES), :], out_hbm.at[tok],
                                         osem.at[sl])
        return pltpu.make_async_copy(obuf.at[sl, pl.ds(r, 1), :], out_hbm.at[pl.ds(tok, 1), :],
                                     osem.at[sl])

    def start_scatter(tile, sl):
        _for_rows(nvalid_ref[tile],
                  lambda r, lane: scatter_copy(tile, sl, r).start(priority=lane % 2))

    def wait_scatter(tile, sl):
        if out_slab:
            part = obuf.at[pl.ds(0, ROW_UNROLL * SUBLANES), :]
        else:
            part = obuf.at[sl, pl.ds(0, ROW_UNROLL), :]
        group = pltpu.make_async_copy(part, part, osem.at[sl])
        _for_rows(nvalid_ref[tile], lambda r, lane: scatter_copy(tile, sl, r).wait(), group.wait)

    def finish_rows(r0, nrows):
        y = _layer_norm(pre_ref[r0:r0 + nrows, :], g_ref[...], b_ref[...])
        if out_slab:
            _slab_store(obuf, pl.multiple_of(other * slab_rows + r0 * SUBLANES, SUBLANES), y)
        else:
            obuf[other, r0:r0 + nrows, :] = y

    @pl.when(i == 0)
    def _():
        gbuf[...] = jnp.zeros_like(gbuf)
        pre_ref[...] = jnp.zeros_like(pre_ref)

    @pl.when((i == 0) & (nused > 0))
    def _():
        start_gather(0, 0)

    @pl.when(i + 1 < nused)
    def _():
        start_gather(i + 1, other)

    @pl.when((i >= 3) & (i <= nused))
    def _():
        wait_scatter(i - 3, other)

    @pl.when(i < nused)
    def _():
        wait_gather(i, slot)
        x = _slab_load(gbuf, pl.multiple_of(slot * slab_rows, SUBLANES), tm)
        xb = x.astype(BF16)
        half = tm // 2
        scores = jax.nn.sigmoid(jnp.dot(xb, rw_ref[...], preferred_element_type=F32))
        lane = lax.broadcasted_iota(jnp.int32, scores.shape, 1)
        sa = jnp.sum(jnp.where(lane == ea_ref[i], scores, 0.0), axis=-1, keepdims=True)
        sb = jnp.sum(jnp.where(lane == eb_ref[i], scores, 0.0), axis=-1, keepdims=True)
        denom = sa + sb
        gt = jnp.dot(xb, wga_ref[...], preferred_element_type=F32)
        finish_rows(0, half)
        up = jnp.dot(xb, wua_ref[...], preferred_element_type=F32)
        finish_rows(half, half)
        hid = (jax.nn.silu(gt) * up).astype(BF16)
        f = (sa / denom) * jnp.dot(hid, wda_ref[...], preferred_element_type=F32)
        gt = jnp.dot(xb, wgb_ref[...], preferred_element_type=F32)
        up = jnp.dot(xb, wub_ref[...], preferred_element_type=F32)
        hid = (jax.nn.silu(gt) * up).astype(BF16)
        f = f + (sb / denom) * jnp.dot(hid, wdb_ref[...], preferred_element_type=F32)
        pre_ref[...] = DEEPNORM_ALPHA * x + f

    @pl.when(i == nused)
    def _():
        finish_rows(0, tm)

    @pl.when((i >= 1) & (i <= nused))
    def _():
        start_scatter(i - 1, other)

    @pl.when(i == nused)
    def _():
        for back in (2, 1):
            tile = i - back

            @pl.when(tile >= 0)
            def _():
                wait_scatter(tile, lax.rem(tile, 2))


def _moe(tok, tile_ea, tile_eb, nvalid, nused, x1, rw_pad, wg, wu, wd, g, b, layer, tm, out_slab):
    steps = tile_ea.shape[0]
    t = x1.shape[0]
    d = SUBLANES * LANES
    ff = wg.shape[-1]
    const = lambda shape: pl.BlockSpec(shape, lambda i, *_: (0,) * len(shape))
    up_a = pl.BlockSpec((None, None, d, ff), lambda i, tok, ea, *_: (layer, ea[i], 0, 0))
    up_b = pl.BlockSpec((None, None, d, ff), lambda i, tok, ea, eb, *_: (layer, eb[i], 0, 0))
    dn_a = pl.BlockSpec((None, None, ff, d), lambda i, tok, ea, *_: (layer, ea[i], 0, 0))
    dn_b = pl.BlockSpec((None, None, ff, d), lambda i, tok, ea, eb, *_: (layer, eb[i], 0, 0))
    if out_slab:
        out_shape = jax.ShapeDtypeStruct((t, SUBLANES, LANES), F32)
        obuf = pltpu.VMEM((2 * tm * SUBLANES, LANES), F32)
    else:
        out_shape = jax.ShapeDtypeStruct((t, d), F32)
        obuf = pltpu.VMEM((2, tm, d), F32)
    return pl.pallas_call(
        functools.partial(_moe_kernel, tm=tm, out_slab=out_slab),
        grid_spec=pltpu.PrefetchScalarGridSpec(
            num_scalar_prefetch=5,
            grid=(steps,),
            in_specs=[
                pl.BlockSpec(memory_space=pl.ANY), const((d, 128)),
                up_a, up_a, dn_a, up_b, up_b, dn_b,
                const((1, d)), const((1, d)),
            ],
            out_specs=pl.BlockSpec(memory_space=pl.ANY),
            scratch_shapes=[
                pltpu.VMEM((2 * tm * SUBLANES, LANES), F32), pltpu.SemaphoreType.DMA((2,)),
                obuf, pltpu.SemaphoreType.DMA((2,)),
                pltpu.VMEM((tm, d), F32),
            ],
        ),
        out_shape=out_shape,
        compiler_params=pltpu.CompilerParams(
            dimension_semantics=("arbitrary",), vmem_limit_bytes=V7X_VMEM_LIMIT),
        name=f"moe_{layer}",
    )(tok, tile_ea, tile_eb, nvalid, nused, x1, rw_pad, wg, wu, wd, wg, wu, wd, g, b)


def _route_and_experts(x1_slab, cls, rw_pad, wg, wu, wd, g, b, layer, tm, out_slab):
    t = cls.shape[1]
    nt = t // tm + N_CLASSES
    ntp = -(-(nt + 1) // 128) * 128
    pos, tile_ea, tile_eb, nvalid, nused = _positions(cls, tm, ntp)
    tok = _invert(pos.reshape(t), nt * tm)
    per_step = lambda a: a.reshape(ntp)[:nt + 1]
    return _moe(tok, per_step(tile_ea), per_step(tile_eb), per_step(nvalid), nused.reshape(128)[:1],
                x1_slab.reshape(t, SUBLANES, LANES), rw_pad, wg, wu, wd, g, b, layer, tm, out_slab)


def _forward(x, a_w_in, a_conv_w, a_w_out, b_w_in, b_ln_g, b_ln_b, b_ws, b_bs, b_w_out,
             router_w, router_bias, moe_w_gate, moe_w_up, moe_w_down,
             ln_mix_g, ln_mix_b, ln_ffn_g, ln_ffn_b, *, ts_conv, ts_sgu, tm):
    bn, s, d = x.shape
    assert d == SUBLANES * LANES
    t = bn * s
    rwt = jnp.concatenate(_split_bf16(router_w.T.astype(F32)), axis=0)
    rb = router_bias.astype(F32).reshape(N_EXPERTS, 1)
    rw_pad = jnp.pad(router_w, ((0, 0), (0, 128 - N_EXPERTS))).astype(BF16)
    wg = moe_w_gate.astype(BF16)
    wu = moe_w_up.astype(BF16)
    wd = moe_w_down.astype(BF16)
    row = lambda a: a.reshape(1, -1)

    x1, cls = _conv_mixer(x, a_w_in[0].astype(BF16), a_conv_w[0], a_w_out[0].astype(BF16),
                          row(ln_mix_g[0]), row(ln_mix_b[0]), rwt, rb, ts_conv)
    x2 = _route_and_experts(x1, cls, rw_pad, wg, wu, wd,
                            row(ln_ffn_g[0]), row(ln_ffn_b[0]), 0, tm, True)
    x3, cls = _sgu_mixer(x2.reshape(t * SUBLANES, LANES), b_w_in[0].astype(BF16),
                         row(b_ln_g[0]), row(b_ln_b[0]), b_ws[0], b_bs[0].T,
                         b_w_out[0].astype(BF16), row(ln_mix_g[1]), row(ln_mix_b[1]), rwt, rb,
                         ts_sgu)
    x4 = _route_and_experts(x3, cls, rw_pad, wg, wu, wd,
                            row(ln_ffn_g[1]), row(ln_ffn_b[1]), 1, tm, False)
    return x4.reshape(bn, s, d)


def kernel(x, a_w_in, a_conv_w, a_w_out, b_w_in, b_ln_g, b_ln_b, b_ws, b_bs, b_w_out, router_w, router_bias, moe_w_gate, moe_w_up, moe_w_down, ln_mix_g, ln_mix_b, ln_ffn_g, ln_ffn_b):
    return _forward(x, a_w_in, a_conv_w, a_w_out, b_w_in, b_ln_g, b_ln_b, b_ws, b_bs, b_w_out,
                    router_w, router_bias, moe_w_gate, moe_w_up, moe_w_down,
                    ln_mix_g, ln_mix_b, ln_ffn_g, ln_ffn_b,
                    ts_conv=512, ts_sgu=256, tm=256)
```

```python
import functools

import jax
import jax.numpy as jnp
from jax import lax
from jax.experimental import pallas as pl
from jax.experimental.pallas import tpu as pltpu

F32 = jnp.float32
BF16 = jnp.bfloat16

N_EXPERTS = 16
N_GROUPS = 4
EXPERTS_PER_GROUP = 4
CHUNK = 128
SGU_HEADS = 8
CONV_WIDTH = 3
DEPTH = 2
DEEPNORM_ALPHA = (2 * DEPTH) ** 0.25
LN_EPS = 1e-5

_PAIRS = ((0, 1), (0, 2), (0, 3), (1, 3), (2, 3), (2, 1))
N_PAIRS = len(_PAIRS)
N_CLASSES = N_GROUPS * N_PAIRS
CLASS_ROWS = 32
_CLASS_EA = tuple(g * EXPERTS_PER_GROUP + p[0] for g in range(N_GROUPS) for p in _PAIRS)
_CLASS_EB = tuple(g * EXPERTS_PER_GROUP + p[1] for g in range(N_GROUPS) for p in _PAIRS)

LANES = 128
SUBLANES = 8
V7X_VMEM_LIMIT = 56 * 1024 * 1024
ROW_UNROLL = 16
INVERT_UNROLL = 16


def _layer_norm(y, g, b):
    mu = jnp.mean(y, axis=-1, keepdims=True)
    d = y - mu
    var = jnp.mean(d * d, axis=-1, keepdims=True)
    return d * lax.rsqrt(var + LN_EPS) * g + b


def _gelu(x):
    return 0.5 * x * (1.0 + lax.erf(x * (2.0 ** -0.5)))


def _slab_load(ref, base, rows):
    return jnp.concatenate(
        [ref[pl.ds(base + s, rows, stride=SUBLANES), :] for s in range(SUBLANES)], axis=1)


def _slab_store(ref, base, val):
    rows = val.shape[0]
    for s in range(SUBLANES):
        ref[pl.ds(base + s, rows, stride=SUBLANES), :] = val[:, s * LANES:(s + 1) * LANES]


def _split_bf16(a):
    hi = a.astype(BF16)
    return hi, (a - hi.astype(F32)).astype(BF16)


def _route_class(x1, rw_split, rb):
    x_hi, x_lo = _split_bf16(x1)
    nt_dims = (((1,), (1,)), ((), ()))
    by_hi = lax.dot_general(rw_split, x_hi, nt_dims, preferred_element_type=F32)
    by_lo = lax.dot_general(rw_split[:N_EXPERTS, :], x_lo, nt_dims, preferred_element_type=F32)
    logits = by_hi[:N_EXPERTS, :] + (by_hi[N_EXPERTS:, :] + by_lo)
    biased = jax.nn.sigmoid(logits) + rb
    v = [biased[e:e + 1, :] for e in range(N_EXPERTS)]
    best_score = None
    best_cls = None
    for g in range(N_GROUPS):
        vg = v[g * EXPERTS_PER_GROUP:(g + 1) * EXPERTS_PER_GROUP]
        sel = []
        for i in range(EXPERTS_PER_GROUP):
            rank = jnp.zeros_like(vg[i], dtype=jnp.int32)
            for j in range(EXPERTS_PER_GROUP):
                if j == i:
                    continue
                beats = (vg[j] >= vg[i]) if j < i else (vg[j] > vg[i])
                rank = rank + beats.astype(jnp.int32)
            sel.append(rank < 2)
        top_sum = None
        cls_g = jnp.zeros_like(vg[0], dtype=jnp.int32)
        for p, (a, b) in enumerate(_PAIRS):
            is_pair = sel[a] & sel[b]
            cls_g = jnp.where(is_pair, g * N_PAIRS + p, cls_g)
            pair_sum = vg[min(a, b)] + vg[max(a, b)]
            top_sum = jnp.where(is_pair, pair_sum, 0.0 if top_sum is None else top_sum)
        if best_score is None:
            best_score, best_cls = top_sum, cls_g
        else:
            better = top_sum > best_score
            best_score = jnp.where(better, top_sum, best_score)
            best_cls = jnp.where(better, cls_g, best_cls)
    return best_cls


def _finish_rows(pre_ref, g_ref, b_ref, rwt_ref, rb_ref, x1_ref, cls_ref, r0, nrows):
    x1 = _layer_norm(pre_ref[r0:r0 + nrows, :], g_ref[...], b_ref[...])
    _slab_store(x1_ref, r0 * SUBLANES, x1)
    cls_ref[:, r0:r0 + nrows] = _route_class(x1, rwt_ref[...], rb_ref[...])


def _conv_mixer_kernel(x_ref, w_in_ref, cw_ref, w_out_ref, g_ref, b_ref, rwt_ref, rb_ref,
                       x1_ref, cls_ref, carry_ref, pre_ref, *, tiles_per_seq):
    ts, d = x_ref.shape
    s = pl.program_id(0)

    @pl.when(s == 0)
    def _():
        pre_ref[...] = jnp.zeros_like(pre_ref)

    @pl.when(lax.rem(s, tiles_per_seq) == 0)
    def _():
        carry_ref[...] = jnp.zeros_like(carry_ref)

    x = x_ref[...]
    xb = x.astype(BF16)
    n_chunks = 2
    piece = 2 * d // n_chunks
    rows_per = ts // n_chunks
    parts = []
    for k in range(n_chunks):
        parts.append(jnp.dot(xb, w_in_ref[:, d + k * piece:d + (k + 1) * piece],
                             preferred_element_type=F32))
        _finish_rows(pre_ref, g_ref, b_ref, rwt_ref, rb_ref, x1_ref, cls_ref,
                     k * rows_per, rows_per)
    gate_h = jnp.concatenate(parts, axis=1)
    ch = gate_h[:, 0:d] * gate_h[:, d:2 * d]
    prev2 = carry_ref[6:7, :]
    prev1 = carry_ref[7:8, :]
    rows = lax.broadcasted_iota(jnp.int32, (ts, 1), 0)
    ch1 = jnp.where(rows == 0, prev1, pltpu.roll(ch, 1, 0))
    ch2 = jnp.where(rows == 0, prev2, jnp.where(rows == 1, prev1, pltpu.roll(ch, 2, 0)))
    z = cw_ref[0:1, :] * ch2 + cw_ref[1:2, :] * ch1 + cw_ref[2:3, :] * ch
    carry_ref[...] = ch[ts - 8:ts, :]
    bg = jnp.dot(xb, w_in_ref[:, 0:d], preferred_element_type=F32)
    m = jnp.dot((bg * z).astype(BF16), w_out_ref[...], preferred_element_type=F32)
    pre_ref[...] = DEEPNORM_ALPHA * x + m


def _conv_mixer(x, w_in, conv_w, w_out, ln_g, ln_b, rwt, rb, ts):
    bn, s, d = x.shape
    ns = s // ts
    n = bn * ns
    const = lambda shape: pl.BlockSpec(shape, lambda i: (0,) * len(shape))
    cur = lambda i: jnp.minimum(i, n - 1)
    prev = lambda i: jnp.maximum(i - 1, 0)
    return pl.pallas_call(
        functools.partial(_conv_mixer_kernel, tiles_per_seq=ns),
        grid=(n + 1,),
        in_specs=[
            pl.BlockSpec((None, ts, d), lambda i: (cur(i) // ns, cur(i) % ns, 0)),
            const((d, 3 * d)), const((CONV_WIDTH, d)), const((d, d)),
            const((1, d)), const((1, d)), const((2 * N_EXPERTS, d)), const((N_EXPERTS, 1)),
        ],
        out_specs=[
            pl.BlockSpec((ts * SUBLANES, LANES), lambda i: (prev(i), 0)),
            pl.BlockSpec((1, ts), lambda i: (0, prev(i))),
        ],
        out_shape=[
            jax.ShapeDtypeStruct((bn * s * SUBLANES, LANES), F32),
            jax.ShapeDtypeStruct((1, bn * s), jnp.int32),
        ],
        scratch_shapes=[pltpu.VMEM((8, d), F32), pltpu.VMEM((ts, d), F32)],
        compiler_params=pltpu.CompilerParams(
            dimension_semantics=("arbitrary",), vmem_limit_bytes=V7X_VMEM_LIMIT),
        name="conv_mixer",
    )(x, w_in, conv_w, w_out, ln_g, ln_b, rwt, rb)


def _sgu_mixer_kernel(xs_ref, w_in_ref, lng_ref, lnb_ref, ws_ref, bst_ref, w_out_ref,
                      g_ref, b_ref, rwt_ref, rb_ref, x1_ref, cls_ref, v_ref, gate_ref, pre_ref):
    ts, width = v_ref.shape
    hd = width // SGU_HEADS

    @pl.when(pl.program_id(0) == 0)
    def _():
        pre_ref[...] = jnp.zeros_like(pre_ref)

    x = _slab_load(xs_ref, 0, ts)
    xb = x.astype(BF16)
    half = width // 2
    v_lo = jnp.dot(xb, w_in_ref[:, width:width + half], preferred_element_type=F32)
    _finish_rows(pre_ref, g_ref, b_ref, rwt_ref, rb_ref, x1_ref, cls_ref, 0, ts)
    v_hi = jnp.dot(xb, w_in_ref[:, width + half:2 * width], preferred_element_type=F32)
    v = _gelu(jnp.concatenate([v_lo, v_hi], axis=1))
    v_ref[...] = _layer_norm(v, lng_ref[...], lnb_ref[...]).astype(BF16)
    r_i = lax.broadcasted_iota(jnp.int32, (CHUNK, CHUNK), 0)
    c_i = lax.broadcasted_iota(jnp.int32, (CHUNK, CHUNK), 1)
    causal = r_i >= c_i
    for h in range(SGU_HEADS):
        cols = slice(h * hd, (h + 1) * hd)
        u_h = _gelu(jnp.dot(xb, w_in_ref[:, cols], preferred_element_type=F32))
        w_h = jnp.where(causal, ws_ref[h], 0.0).astype(BF16)
        bias = bst_ref[:, h:h + 1]
        for c in range(ts // CHUNK):
            rws = slice(c * CHUNK, (c + 1) * CHUNK)
            mixed = jnp.dot(w_h, v_ref[rws, cols], preferred_element_type=F32) + bias
            gate_ref[rws, cols] = (u_h[rws, :] * mixed).astype(BF16)
    m = jnp.dot(gate_ref[...], w_out_ref[...], preferred_element_type=F32)
    pre_ref[...] = DEEPNORM_ALPHA * x + m


def _sgu_mixer(xs, w_in, ln_g, ln_b, ws, bst, w_out, g, b, rwt, rb, ts):
    t = xs.shape[0] // SUBLANES
    n = t // ts
    width, d = w_out.shape
    const = lambda shape: pl.BlockSpec(shape, lambda i: (0,) * len(shape))
    cur = lambda i: jnp.minimum(i, n - 1)
    prev = lambda i: jnp.maximum(i - 1, 0)
    return pl.pallas_call(
        _sgu_mixer_kernel,
        grid=(n + 1,),
        in_specs=[
            pl.BlockSpec((ts * SUBLANES, LANES), lambda i: (cur(i), 0)),
            const((d, 2 * width)), const((1, width)), const((1, width)),
            const((SGU_HEADS, CHUNK, CHUNK)), const((CHUNK, SGU_HEADS)), const((width, d)),
            const((1, d)), const((1, d)), const((2 * N_EXPERTS, d)), const((N_EXPERTS, 1)),
        ],
        out_specs=[
            pl.BlockSpec((ts * SUBLANES, LANES), lambda i: (prev(i), 0)),
            pl.BlockSpec((1, ts), lambda i: (0, prev(i))),
        ],
        out_shape=[
            jax.ShapeDtypeStruct((t * SUBLANES, LANES), F32),
            jax.ShapeDtypeStruct((1, t), jnp.int32),
        ],
        scratch_shapes=[pltpu.VMEM((ts, width), BF16), pltpu.VMEM((ts, width), BF16),
                        pltpu.VMEM((ts, d), F32)],
        compiler_params=pltpu.CompilerParams(
            dimension_semantics=("arbitrary",), vmem_limit_bytes=V7X_VMEM_LIMIT),
        name="sgu_mixer",
    )(xs, w_in, ln_g, ln_b, ws, bst, w_out, g, b, rwt, rb)


def _positions_kernel(cls_ref, pos_ref, tile_ea_ref, tile_eb_ref, nvalid_ref, nused_ref,
                      incl_ref, *, tm):
    t = cls_ref.shape[1]
    ntp = tile_ea_ref.shape[1]
    lanes = 256
    cls = cls_ref[...]
    crow = lax.broadcasted_iota(jnp.int32, (CLASS_ROWS, t), 0)
    onehot = (crow == cls).astype(F32)
    k_i = lax.broadcasted_iota(jnp.int32, (lanes, lanes), 0)
    j_i = lax.broadcasted_iota(jnp.int32, (lanes, lanes), 1)
    upper = (k_i <= j_i).astype(BF16)
    count = jnp.zeros((CLASS_ROWS, 1), F32)
    for c in range(t // lanes):
        sl = slice(c * lanes, (c + 1) * lanes)
        inc = jnp.dot(onehot[:, sl].astype(BF16), upper, preferred_element_type=F32) + count
        incl_ref[:, sl] = inc
        count = inc[:, lanes - 1:lanes]
    rank = jnp.sum(onehot * incl_ref[...], axis=0, keepdims=True) - 1.0
    ntile = jnp.floor((count + (tm - 1)) * (1.0 / tm))
    srow = lax.broadcasted_iota(jnp.int32, (CLASS_ROWS, 1), 0)
    tstart = jnp.zeros((CLASS_ROWS, 1), F32)
    for c in range(N_CLASSES):
        tstart = tstart + jnp.where(srow > c, ntile[c:c + 1, :], 0.0)
    tend = tstart + ntile
    pos = jnp.sum(onehot * (tstart * tm), axis=0, keepdims=True) + rank
    pos_ref[...] = pos.astype(jnp.int32)
    nused = jnp.max(jnp.where(srow < N_CLASSES, tend, 0.0), axis=0, keepdims=True)
    last_cls = jnp.max(jnp.where((ntile > 0) & (srow < N_CLASSES), srow, 0), axis=0, keepdims=True)
    tile_i = lax.broadcasted_iota(jnp.int32, (CLASS_ROWS, ntp), 1).astype(F32)
    is_cls = lax.broadcasted_iota(jnp.int32, (CLASS_ROWS, ntp), 0) < N_CLASSES
    done = (tile_i >= tend) & is_cls
    tcls = jnp.minimum(jnp.sum(done.astype(jnp.int32), axis=0, keepdims=True), last_cls)
    ea = jnp.zeros((1, ntp), jnp.int32)
    eb = jnp.zeros((1, ntp), jnp.int32)
    for c in range(N_CLASSES):
        ea = jnp.where(tcls == c, _CLASS_EA[c], ea)
        eb = jnp.where(tcls == c, _CLASS_EB[c], eb)
    tile_ea_ref[...] = ea
    tile_eb_ref[...] = eb
    inside = (tile_i >= tstart) & (tile_i < tend) & is_cls
    left = jnp.minimum(count - (tile_i - tstart) * tm, float(tm))
    nvalid_ref[...] = jnp.sum(jnp.where(inside, left, 0.0), axis=0, keepdims=True).astype(jnp.int32)
    nused_ref[...] = jnp.broadcast_to(nused, nused_ref.shape).astype(jnp.int32)


def _positions(cls, tm, ntp):
    t = cls.shape[1]
    return pl.pallas_call(
        functools.partial(_positions_kernel, tm=tm),
        out_shape=[
            jax.ShapeDtypeStruct((1, t), jnp.int32),
            jax.ShapeDtypeStruct((1, ntp), jnp.int32),
            jax.ShapeDtypeStruct((1, ntp), jnp.int32),
            jax.ShapeDtypeStruct((1, ntp), jnp.int32),
            jax.ShapeDtypeStruct((1, 128), jnp.int32),
        ],
        scratch_shapes=[pltpu.VMEM((CLASS_ROWS, t), F32)],
        compiler_params=pltpu.CompilerParams(vmem_limit_bytes=V7X_VMEM_LIMIT),
        name="positions",
    )(cls)


def _invert_kernel(pos_ref, tok_ref):
    t = pos_ref.shape[0]
    r = tok_ref.shape[0]

    def fill(j, carry):
        for u in range(INVERT_UNROLL):
            tok_ref[j * INVERT_UNROLL + u] = 0
        return carry

    lax.fori_loop(0, r // INVERT_UNROLL, fill, 0)

    def put(j, carry):
        k0 = j * INVERT_UNROLL
        dst = [pos_ref[k0 + u] for u in range(INVERT_UNROLL)]
        for u in range(INVERT_UNROLL):
            tok_ref[dst[u]] = k0 + u
        return carry

    lax.fori_loop(0, t // INVERT_UNROLL, put, 0)


def _invert(pos, r):
    return pl.pallas_call(
        _invert_kernel,
        in_specs=[pl.BlockSpec(memory_space=pltpu.SMEM)],
        out_specs=pl.BlockSpec(memory_space=pltpu.SMEM),
        out_shape=jax.ShapeDtypeStruct((r,), jnp.int32),
        name="invert_positions",
    )(pos)


def _for_rows(n, per_row, per_group=None):
    ngroups = n // ROW_UNROLL

    def group(j, carry):
        if per_group is not None:
            per_group()
        else:
            for u in range(ROW_UNROLL):
                per_row(j * ROW_UNROLL + u)
        return carry

    lax.fori_loop(0, ngroups, group, 0)

    def single(r, carry):
        per_row(r)
        return carry

    lax.fori_loop(ngroups * ROW_UNROLL, n, single, 0)


def _moe_kernel(tok_ref, ea_ref, eb_ref, nvalid_ref, nused_ref, x_hbm, rw_ref, wga_ref, wua_ref,
                wda_ref, wgb_ref, wub_ref, wdb_ref, g_ref, b_ref, out_hbm, gbuf, gsem, obuf, osem,
                pre_ref, *, tm, out_slab):
    i = pl.program_id(0)
    nused = nused_ref[0]
    slot = lax.rem(i, 2)
    other = 1 - slot
    slab_rows = tm * SUBLANES

    def gather_copy(tile, sl, r):
        dst = pl.multiple_of((sl * tm + r) * SUBLANES, SUBLANES)
        return pltpu.make_async_copy(x_hbm.at[tok_ref[tile * tm + r]],
                                     gbuf.at[pl.ds(dst, SUBLANES), :], gsem.at[sl])

    def start_gather(tile, sl):
        _for_rows(nvalid_ref[tile], lambda r: gather_copy(tile, sl, r).start())

    def wait_gather(tile, sl):
        group = pltpu.make_async_copy(gbuf.at[pl.ds(0, ROW_UNROLL * SUBLANES), :],
                                      gbuf.at[pl.ds(0, ROW_UNROLL * SUBLANES), :], gsem.at[sl])
        _for_rows(nvalid_ref[tile], lambda r: gather_copy(tile, sl, r).wait(), group.wait)

    def scatter_copy(tile, sl, r):
        tok = tok_ref[tile * tm + r]
        if out_slab:
            src = pl.multiple_of((sl * tm + r) * SUBLANES, SUBLANES)
            return pltpu.make_async_copy(obuf.at[pl.ds(src, SUBLANES), :], out_hbm.at[tok],
                                         osem.at[sl])
        return pltpu.make_async_copy(obuf.at[sl, pl.ds(r, 1), :], out_hbm.at[pl.ds(tok, 1), :],
                                     osem.at[sl])

    def start_scatter(tile, sl):
        _for_rows(nvalid_ref[tile], lambda r: scatter_copy(tile, sl, r).start())

    def wait_scatter(tile, sl):
        if out_slab:
            part = obuf.at[pl.ds(0, ROW_UNROLL * SUBLANES), :]
        else:
            part = obuf.at[sl, pl.ds(0, ROW_UNROLL), :]
        group = pltpu.make_async_copy(part, part, osem.at[sl])
        _for_rows(nvalid_ref[tile], lambda r: scatter_copy(tile, sl, r).wait(), group.wait)

    def finish_rows(r0, nrows):
        y = _layer_norm(pre_ref[r0:r0 + nrows, :], g_ref[...], b_ref[...])
        if out_slab:
            _slab_store(obuf, pl.multiple_of(other * slab_rows + r0 * SUBLANES, SUBLANES), y)
        else:
            obuf[other, r0:r0 + nrows, :] = y

    @pl.when(i == 0)
    def _():
        gbuf[...] = jnp.zeros_like(gbuf)
        pre_ref[...] = jnp.zeros_like(pre_ref)

    @pl.when((i == 0) & (nused > 0))
    def _():
        start_gather(0, 0)

    @pl.when(i + 1 < nused)
    def _():
        start_gather(i + 1, other)

    @pl.when((i >= 3) & (i <= nused))
    def _():
        wait_scatter(i - 3, other)

    @pl.when(i < nused)
    def _():
        wait_gather(i, slot)
        x = _slab_load(gbuf, pl.multiple_of(slot * slab_rows, SUBLANES), tm)
        xb = x.astype(BF16)
        half = tm // 2
        scores = jax.nn.sigmoid(jnp.dot(xb, rw_ref[...], preferred_element_type=F32))
        lane = lax.broadcasted_iota(jnp.int32, scores.shape, 1)
        sa = jnp.sum(jnp.where(lane == ea_ref[i], scores, 0.0), axis=-1, keepdims=True)
        sb = jnp.sum(jnp.where(lane == eb_ref[i], scores, 0.0), axis=-1, keepdims=True)
        denom = sa + sb
        gt = jnp.dot(xb, wga_ref[...], preferred_element_type=F32)
        finish_rows(0, half)
        up = jnp.dot(xb, wua_ref[...], preferred_element_type=F32)
        finish_rows(half, half)
        hid = (jax.nn.silu(gt) * up).astype(BF16)
        f = (sa / denom) * jnp.dot(hid, wda_ref[...], preferred_element_type=F32)
        gt = jnp.dot(xb, wgb_ref[...], preferred_element_type=F32)
        up = jnp.dot(xb, wub_ref[...], preferred_element_type=F32)
        hid = (jax.nn.silu(gt) * up).astype(BF16)
        f = f + (sb / denom) * jnp.dot(hid, wdb_ref[...], preferred_element_type=F32)
        pre_ref[...] = DEEPNORM_ALPHA * x + f

    @pl.when(i == nused)
    def _():
        finish_rows(0, tm)

    @pl.when((i >= 1) & (i <= nused))
    def _():
        start_scatter(i - 1, other)

    @pl.when(i == nused)
    def _():
        for back in (2, 1):
            tile = i - back

            @pl.when(tile >= 0)
            def _():
                wait_scatter(tile, lax.rem(tile, 2))


def _moe(tok, tile_ea, tile_eb, nvalid, nused, x1, rw_pad, wg, wu, wd, g, b, layer, tm, out_slab):
    steps = tile_ea.shape[0]
    t = x1.shape[0]
    d = SUBLANES * LANES
    ff = wg.shape[-1]
    const = lambda shape: pl.BlockSpec(shape, lambda i, *_: (0,) * len(shape))
    up_a = pl.BlockSpec((None, None, d, ff), lambda i, tok, ea, *_: (layer, ea[i], 0, 0))
    up_b = pl.BlockSpec((None, None, d, ff), lambda i, tok, ea, eb, *_: (layer, eb[i], 0, 0))
    dn_a = pl.BlockSpec((None, None, ff, d), lambda i, tok, ea, *_: (layer, ea[i], 0, 0))
    dn_b = pl.BlockSpec((None, None, ff, d), lambda i, tok, ea, eb, *_: (layer, eb[i], 0, 0))
    if out_slab:
        out_shape = jax.ShapeDtypeStruct((t, SUBLANES, LANES), F32)
        obuf = pltpu.VMEM((2 * tm * SUBLANES, LANES), F32)
    else:
        out_shape = jax.ShapeDtypeStruct((t, d), F32)
        obuf = pltpu.VMEM((2, tm, d), F32)
    return pl.pallas_call(
        functools.partial(_moe_kernel, tm=tm, out_slab=out_slab),
        grid_spec=pltpu.PrefetchScalarGridSpec(
            num_scalar_prefetch=5,
            grid=(steps,),
            in_specs=[
                pl.BlockSpec(memory_space=pl.ANY), const((d, 128)),
                up_a, up_a, dn_a, up_b, up_b, dn_b,
                const((1, d)), const((1, d)),
            ],
            out_specs=pl.BlockSpec(memory_space=pl.ANY),
            scratch_shapes=[
                pltpu.VMEM((2 * tm * SUBLANES, LANES), F32), pltpu.SemaphoreType.DMA((2,)),
                obuf, pltpu.SemaphoreType.DMA((2,)),
                pltpu.VMEM((tm, d), F32),
            ],
        ),
        out_shape=out_shape,
        compiler_params=pltpu.CompilerParams(
            dimension_semantics=("arbitrary",), vmem_limit_bytes=V7X_VMEM_LIMIT),
        name=f"moe_{layer}",
    )(tok, tile_ea, tile_eb, nvalid, nused, x1, rw_pad, wg, wu, wd, wg, wu, wd, g, b)


def _route_and_experts(x1_slab, cls, rw_pad, wg, wu, wd, g, b, layer, tm, out_slab):
    t = cls.shape[1]
    nt = t // tm + N_CLASSES
    ntp = -(-(nt + 1) // 128) * 128
    pos, tile_ea, tile_eb, nvalid, nused = _positions(cls, tm, ntp)
    tok = _invert(pos.reshape(t), nt * tm)
    per_step = lambda a: a.reshape(ntp)[:nt + 1]
    return _moe(tok, per_step(tile_ea), per_step(tile_eb), per_step(nvalid), nused.reshape(128)[:1],
                x1_slab.reshape(t, SUBLANES, LANES), rw_pad, wg, wu, wd, g, b, layer, tm, out_slab)


def _forward(x, a_w_in, a_conv_w, a_w_out, b_w_in, b_ln_g, b_ln_b, b_ws, b_bs, b_w_out,
             router_w, router_bias, moe_w_gate, moe_w_up, moe_w_down,
             ln_mix_g, ln_mix_b, ln_ffn_g, ln_ffn_b, *, ts_conv, ts_sgu, tm):
    bn, s, d = x.shape
    assert d == SUBLANES * LANES
    t = bn * s
    rwt = jnp.concatenate(_split_bf16(router_w.T.astype(F32)), axis=0)
    rb = router_bias.astype(F32).reshape(N_EXPERTS, 1)
    rw_pad = jnp.pad(router_w, ((0, 0), (0, 128 - N_EXPERTS))).astype(BF16)
    wg = moe_w_gate.astype(BF16)
    wu = moe_w_up.astype(BF16)
    wd = moe_w_down.astype(BF16)
    row = lambda a: a.reshape(1, -1)

    x1, cls = _conv_mixer(x, a_w_in[0].astype(BF16), a_conv_w[0], a_w_out[0].astype(BF16),
                          row(ln_mix_g[0]), row(ln_mix_b[0]), rwt, rb, ts_conv)
    x2 = _route_and_experts(x1, cls, rw_pad, wg, wu, wd,
                            row(ln_ffn_g[0]), row(ln_ffn_b[0]), 0, tm, True)
    x3, cls = _sgu_mixer(x2.reshape(t * SUBLANES, LANES), b_w_in[0].astype(BF16),
                         row(b_ln_g[0]), row(b_ln_b[0]), b_ws[0], b_bs[0].T,
                         b_w_out[0].astype(BF16), row(ln_mix_g[1]), row(ln_mix_b[1]), rwt, rb,
                         ts_sgu)
    x4 = _route_and_experts(x3, cls, rw_pad, wg, wu, wd,
                            row(ln_ffn_g[1]), row(ln_ffn_b[1]), 1, tm, False)
    return x4.reshape(bn, s, d)


def kernel(x, a_w_in, a_conv_w, a_w_out, b_w_in, b_ln_g, b_ln_b, b_ws, b_bs, b_w_out, router_w, router_bias, moe_w_gate, moe_w_up, moe_w_down, ln_mix_g, ln_mix_b, ln_ffn_g, ln_ffn_b):
    return _forward(x, a_w_in, a_conv_w, a_w_out, b_w_in, b_ln_g, b_ln_b, b_ws, b_bs, b_w_out,
                    router_w, router_bias, moe_w_gate, moe_w_up, moe_w_down,
                    ln_mix_g, ln_mix_b, ln_ffn_g, ln_ffn_b,
                    ts_conv=512, ts_sgu=256, tm=256)
```

```python
import functools

import jax
import jax.numpy as jnp
from jax import lax
from jax.experimental import pallas as pl
from jax.experimental.pallas import tpu as pltpu

F32 = jnp.float32
BF16 = jnp.bfloat16

N_EXPERTS = 16
N_GROUPS = 4
EXPERTS_PER_GROUP = 4
CHUNK = 128
SGU_HEADS = 8
CONV_WIDTH = 3
DEPTH = 2
DEEPNORM_ALPHA = (2 * DEPTH) ** 0.25
LN_EPS = 1e-5

_PAIRS = ((0, 1), (0, 2), (0, 3), (1, 3), (2, 3), (2, 1))
N_PAIRS = len(_PAIRS)
N_CLASSES = N_GROUPS * N_PAIRS
CLASS_ROWS = 32
_CLASS_EA = tuple(g * EXPERTS_PER_GROUP + p[0] for g in range(N_GROUPS) for p in _PAIRS)
_CLASS_EB = tuple(g * EXPERTS_PER_GROUP + p[1] for g in range(N_GROUPS) for p in _PAIRS)

LANES = 128
SUBLANES = 8
V7X_VMEM_LIMIT = 56 * 1024 * 1024
ROW_UNROLL = 16
INVERT_UNROLL = 16


def _layer_norm(y, g, b):
    mu = jnp.mean(y, axis=-1, keepdims=True)
    d = y - mu
    var = jnp.mean(d * d, axis=-1, keepdims=True)
    return d * lax.rsqrt(var + LN_EPS) * g + b


def _gelu(x):
    return 0.5 * x * (1.0 + lax.erf(x * (2.0 ** -0.5)))


def _slab_load(ref, base, rows):
    return jnp.concatenate(
        [ref[pl.ds(base + s, rows, stride=SUBLANES), :] for s in range(SUBLANES)], axis=1)


def _slab_store(ref, base, val):
    rows = val.shape[0]
    for s in range(SUBLANES):
        ref[pl.ds(base + s, rows, stride=SUBLANES), :] = val[:, s * LANES:(s + 1) * LANES]


def _split_bf16(a):
    hi = a.astype(BF16)
    return hi, (a - hi.astype(F32)).astype(BF16)


def _route_class(x1, rw_split, rb):
    x_hi, x_lo = _split_bf16(x1)
    nt_dims = (((1,), (1,)), ((), ()))
    by_hi = lax.dot_general(rw_split, x_hi, nt_dims, preferred_element_type=F32)
    by_lo = lax.dot_general(rw_split[:N_EXPERTS, :], x_lo, nt_dims, preferred_element_type=F32)
    logits = by_hi[:N_EXPERTS, :] + (by_hi[N_EXPERTS:, :] + by_lo)
    biased = jax.nn.sigmoid(logits) + rb
    v = [biased[e:e + 1, :] for e in range(N_EXPERTS)]
    best_score = None
    best_cls = None
    for g in range(N_GROUPS):
        vg = v[g * EXPERTS_PER_GROUP:(g + 1) * EXPERTS_PER_GROUP]
        sel = []
        for i in range(EXPERTS_PER_GROUP):
            rank = jnp.zeros_like(vg[i], dtype=jnp.int32)
            for j in range(EXPERTS_PER_GROUP):
                if j == i:
                    continue
                beats = (vg[j] >= vg[i]) if j < i else (vg[j] > vg[i])
                rank = rank + beats.astype(jnp.int32)
            sel.append(rank < 2)
        top_sum = None
        cls_g = jnp.zeros_like(vg[0], dtype=jnp.int32)
        for p, (a, b) in enumerate(_PAIRS):
            is_pair = sel[a] & sel[b]
            cls_g = jnp.where(is_pair, g * N_PAIRS + p, cls_g)
            pair_sum = vg[min(a, b)] + vg[max(a, b)]
            top_sum = jnp.where(is_pair, pair_sum, 0.0 if top_sum is None else top_sum)
        if best_score is None:
            best_score, best_cls = top_sum, cls_g
        else:
            better = top_sum > best_score
            best_score = jnp.where(better, top_sum, best_score)
            best_cls = jnp.where(better, cls_g, best_cls)
    return best_cls


def _finish_rows(pre_ref, g_ref, b_ref, rwt_ref, rb_ref, x1_ref, cls_ref, r0, nrows):
    x1 = _layer_norm(pre_ref[r0:r0 + nrows, :], g_ref[...], b_ref[...])
    _slab_store(x1_ref, r0 * SUBLANES, x1)
    cls_ref[:, r0:r0 + nrows] = _route_class(x1, rwt_ref[...], rb_ref[...])


def _conv_mixer_kernel(x_ref, w_in_ref, cw_ref, w_out_ref, g_ref, b_ref, rwt_ref, rb_ref,
                       x1_ref, cls_ref, carry_ref, pre_ref, *, tiles_per_seq):
    ts, d = x_ref.shape
    s = pl.program_id(0)

    @pl.when(s == 0)
    def _():
        pre_ref[...] = jnp.zeros_like(pre_ref)

    @pl.when(lax.rem(s, tiles_per_seq) == 0)
    def _():
        carry_ref[...] = jnp.zeros_like(carry_ref)

    x = x_ref[...]
    xb = x.astype(BF16)
    n_chunks = 2
    piece = 2 * d // n_chunks
    rows_per = ts // n_chunks
    parts = []
    for k in range(n_chunks):
        parts.append(jnp.dot(xb, w_in_ref[:, d + k * piece:d + (k + 1) * piece],
                             preferred_element_type=F32))
        _finish_rows(pre_ref, g_ref, b_ref, rwt_ref, rb_ref, x1_ref, cls_ref,
                     k * rows_per, rows_per)
    gate_h = jnp.concatenate(parts, axis=1)
    ch = gate_h[:, 0:d] * gate_h[:, d:2 * d]
    prev2 = carry_ref[6:7, :]
    prev1 = carry_ref[7:8, :]
    rows = lax.broadcasted_iota(jnp.int32, (SUBLANES, 1), 0)
    ch1 = pltpu.roll(ch, 1, 0)
    ch2 = pltpu.roll(ch, 2, 0)
    head1 = jnp.where(rows == 0, prev1, ch1[0:SUBLANES, :])
    head2 = jnp.where(rows == 0, prev2, jnp.where(rows == 1, prev1, ch2[0:SUBLANES, :]))
    ch1 = jnp.concatenate([head1, ch1[SUBLANES:, :]], axis=0)
    ch2 = jnp.concatenate([head2, ch2[SUBLANES:, :]], axis=0)
    z = cw_ref[0:1, :] * ch2 + cw_ref[1:2, :] * ch1 + cw_ref[2:3, :] * ch
    carry_ref[...] = ch[ts - 8:ts, :]
    bg = jnp.dot(xb, w_in_ref[:, 0:d], preferred_element_type=F32)
    m = jnp.dot((bg * z).astype(BF16), w_out_ref[...], preferred_element_type=F32)
    pre_ref[...] = DEEPNORM_ALPHA * x + m


def _conv_mixer(x, w_in, conv_w, w_out, ln_g, ln_b, rwt, rb, ts):
    bn, s, d = x.shape
    ns = s // ts
    n = bn * ns
    const = lambda shape: pl.BlockSpec(shape, lambda i: (0,) * len(shape))
    cur = lambda i: jnp.minimum(i, n - 1)
    prev = lambda i: jnp.maximum(i - 1, 0)
    return pl.pallas_call(
        functools.partial(_conv_mixer_kernel, tiles_per_seq=ns),
        grid=(n + 1,),
        in_specs=[
            pl.BlockSpec((None, ts, d), lambda i: (cur(i) // ns, cur(i) % ns, 0)),
            const((d, 3 * d)), const((CONV_WIDTH, d)), const((d, d)),
            const((1, d)), const((1, d)), const((2 * N_EXPERTS, d)), const((N_EXPERTS, 1)),
        ],
        out_specs=[
            pl.BlockSpec((ts * SUBLANES, LANES), lambda i: (prev(i), 0)),
            pl.BlockSpec((1, ts), lambda i: (0, prev(i))),
        ],
        out_shape=[
            jax.ShapeDtypeStruct((bn * s * SUBLANES, LANES), F32),
            jax.ShapeDtypeStruct((1, bn * s), jnp.int32),
        ],
        scratch_shapes=[pltpu.VMEM((8, d), F32), pltpu.VMEM((ts, d), F32)],
        compiler_params=pltpu.CompilerParams(
            dimension_semantics=("arbitrary",), vmem_limit_bytes=V7X_VMEM_LIMIT),
        name="conv_mixer",
    )(x, w_in, conv_w, w_out, ln_g, ln_b, rwt, rb)


def _sgu_mixer_kernel(xs_ref, w_in_ref, lng_ref, lnb_ref, ws_ref, bst_ref, w_out_ref,
                      g_ref, b_ref, rwt_ref, rb_ref, x1_ref, cls_ref, v_ref, gate_ref, pre_ref):
    ts, width = v_ref.shape
    hd = width // SGU_HEADS

    @pl.when(pl.program_id(0) == 0)
    def _():
        pre_ref[...] = jnp.zeros_like(pre_ref)

    x = _slab_load(xs_ref, 0, ts)
    xb = x.astype(BF16)
    half = width // 2
    v_lo = jnp.dot(xb, w_in_ref[:, width:width + half], preferred_element_type=F32)
    _finish_rows(pre_ref, g_ref, b_ref, rwt_ref, rb_ref, x1_ref, cls_ref, 0, ts)
    v_hi = jnp.dot(xb, w_in_ref[:, width + half:2 * width], preferred_element_type=F32)
    v = _gelu(jnp.concatenate([v_lo, v_hi], axis=1))
    v_ref[...] = _layer_norm(v, lng_ref[...], lnb_ref[...]).astype(BF16)
    r_i = lax.broadcasted_iota(jnp.int32, (CHUNK, CHUNK), 0)
    c_i = lax.broadcasted_iota(jnp.int32, (CHUNK, CHUNK), 1)
    causal = r_i >= c_i
    for h in range(SGU_HEADS):
        cols = slice(h * hd, (h + 1) * hd)
        u_h = _gelu(jnp.dot(xb, w_in_ref[:, cols], preferred_element_type=F32))
        w_h = jnp.where(causal, ws_ref[h], 0.0).astype(BF16)
        bias = bst_ref[:, h:h + 1]
        for c in range(ts // CHUNK):
            rws = slice(c * CHUNK, (c + 1) * CHUNK)
            mixed = jnp.dot(w_h, v_ref[rws, cols], preferred_element_type=F32) + bias
            gate_ref[rws, cols] = (u_h[rws, :] * mixed).astype(BF16)
    m = jnp.dot(gate_ref[...], w_out_ref[...], preferred_element_type=F32)
    pre_ref[...] = DEEPNORM_ALPHA * x + m


def _sgu_mixer(xs, w_in, ln_g, ln_b, ws, bst, w_out, g, b, rwt, rb, ts):
    t = xs.shape[0] // SUBLANES
    n = t // ts
    width, d = w_out.shape
    const = lambda shape: pl.BlockSpec(shape, lambda i: (0,) * len(shape))
    cur = lambda i: jnp.minimum(i, n - 1)
    prev = lambda i: jnp.maximum(i - 1, 0)
    return pl.pallas_call(
        _sgu_mixer_kernel,
        grid=(n + 1,),
        in_specs=[
            pl.BlockSpec((ts * SUBLANES, LANES), lambda i: (cur(i), 0)),
            const((d, 2 * width)), const((1, width)), const((1, width)),
            const((SGU_HEADS, CHUNK, CHUNK)), const((CHUNK, SGU_HEADS)), const((width, d)),
            const((1, d)), const((1, d)), const((2 * N_EXPERTS, d)), const((N_EXPERTS, 1)),
        ],
        out_specs=[
            pl.BlockSpec((ts * SUBLANES, LANES), lambda i: (prev(i), 0)),
            pl.BlockSpec((1, ts), lambda i: (0, prev(i))),
        ],
        out_shape=[
            jax.ShapeDtypeStruct((t * SUBLANES, LANES), F32),
            jax.ShapeDtypeStruct((1, t), jnp.int32),
        ],
        scratch_shapes=[pltpu.VMEM((ts, width), BF16), pltpu.VMEM((ts, width), BF16),
                        pltpu.VMEM((ts, d), F32)],
        compiler_params=pltpu.CompilerParams(
            dimension_semantics=("arbitrary",), vmem_limit_bytes=V7X_VMEM_LIMIT),
        name="sgu_mixer",
    )(xs, w_in, ln_g, ln_b, ws, bst, w_out, g, b, rwt, rb)


def _positions_kernel(cls_ref, pos_ref, tile_ea_ref, tile_eb_ref, nvalid_ref, nused_ref,
                      incl_ref, *, tm):
    t = cls_ref.shape[1]
    ntp = tile_ea_ref.shape[1]
    lanes = 256
    cls = cls_ref[...]
    crow = lax.broadcasted_iota(jnp.int32, (CLASS_ROWS, t), 0)
    onehot = (crow == cls).astype(F32)
    k_i = lax.broadcasted_iota(jnp.int32, (lanes, lanes), 0)
    j_i = lax.broadcasted_iota(jnp.int32, (lanes, lanes), 1)
    upper = (k_i <= j_i).astype(BF16)
    count = jnp.zeros((CLASS_ROWS, 1), F32)
    for c in range(t // lanes):
        sl = slice(c * lanes, (c + 1) * lanes)
        inc = jnp.dot(onehot[:, sl].astype(BF16), upper, preferred_element_type=F32) + count
        incl_ref[:, sl] = inc
        count = inc[:, lanes - 1:lanes]
    rank = jnp.sum(onehot * incl_ref[...], axis=0, keepdims=True) - 1.0
    ntile = jnp.floor((count + (tm - 1)) * (1.0 / tm))
    srow = lax.broadcasted_iota(jnp.int32, (CLASS_ROWS, 1), 0)
    tstart = jnp.zeros((CLASS_ROWS, 1), F32)
    for c in range(N_CLASSES):
        tstart = tstart + jnp.where(srow > c, ntile[c:c + 1, :], 0.0)
    tend = tstart + ntile
    pos = jnp.sum(onehot * (tstart * tm), axis=0, keepdims=True) + rank
    pos_ref[...] = pos.astype(jnp.int32)
    nused = jnp.max(jnp.where(srow < N_CLASSES, tend, 0.0), axis=0, keepdims=True)
    last_cls = jnp.max(jnp.where((ntile > 0) & (srow < N_CLASSES), srow, 0), axis=0, keepdims=True)
    tile_i = lax.broadcasted_iota(jnp.int32, (CLASS_ROWS, ntp), 1).astype(F32)
    is_cls = lax.broadcasted_iota(jnp.int32, (CLASS_ROWS, ntp), 0) < N_CLASSES
    done = (tile_i >= tend) & is_cls
    tcls = jnp.minimum(jnp.sum(done.astype(jnp.int32), axis=0, keepdims=True), last_cls)
    ea = jnp.zeros((1, ntp), jnp.int32)
    eb = jnp.zeros((1, ntp), jnp.int32)
    for c in range(N_CLASSES):
        ea = jnp.where(tcls == c, _CLASS_EA[c], ea)
        eb = jnp.where(tcls == c, _CLASS_EB[c], eb)
    tile_ea_ref[...] = ea
    tile_eb_ref[...] = eb
    inside = (tile_i >= tstart) & (tile_i < tend) & is_cls
    left = jnp.minimum(count - (tile_i - tstart) * tm, float(tm))
    nvalid_ref[...] = jnp.sum(jnp.where(inside, left, 0.0), axis=0, keepdims=True).astype(jnp.int32)
    nused_ref[...] = jnp.broadcast_to(nused, nused_ref.shape).astype(jnp.int32)


def _positions(cls, tm, ntp):
    t = cls.shape[1]
    return pl.pallas_call(
        functools.partial(_positions_kernel, tm=tm),
        out_shape=[
            jax.ShapeDtypeStruct((1, t), jnp.int32),
            jax.ShapeDtypeStruct((1, ntp), jnp.int32),
            jax.ShapeDtypeStruct((1, ntp), jnp.int32),
            jax.ShapeDtypeStruct((1, ntp), jnp.int32),
            jax.ShapeDtypeStruct((1, 128), jnp.int32),
        ],
        scratch_shapes=[pltpu.VMEM((CLASS_ROWS, t), F32)],
        compiler_params=pltpu.CompilerParams(vmem_limit_bytes=V7X_VMEM_LIMIT),
        name="positions",
    )(cls)


def _invert_kernel(pos_ref, tok_ref):
    t = pos_ref.shape[0]
    r = tok_ref.shape[0]

    def fill(j, carry):
        for u in range(INVERT_UNROLL):
            tok_ref[j * INVERT_UNROLL + u] = 0
        return carry

    lax.fori_loop(0, r // INVERT_UNROLL, fill, 0)

    def put(j, carry):
        k0 = j * INVERT_UNROLL
        dst = [pos_ref[k0 + u] for u in range(INVERT_UNROLL)]
        for u in range(INVERT_UNROLL):
            tok_ref[dst[u]] = k0 + u
        return carry

    lax.fori_loop(0, t // INVERT_UNROLL, put, 0)


def _invert(pos, r):
    return pl.pallas_call(
        _invert_kernel,
        in_specs=[pl.BlockSpec(memory_space=pltpu.SMEM)],
        out_specs=pl.BlockSpec(memory_space=pltpu.SMEM),
        out_shape=jax.ShapeDtypeStruct((r,), jnp.int32),
        name="invert_positions",
    )(pos)


def _for_rows(n, per_row, per_group=None):
    ngroups = n // ROW_UNROLL

    def group(j, carry):
        if per_group is not None:
            per_group()
        else:
            for u in range(ROW_UNROLL):
                per_row(j * ROW_UNROLL + u)
        return carry

    lax.fori_loop(0, ngroups, group, 0)

    def single(r, carry):
        per_row(r)
        return carry

    lax.fori_loop(ngroups * ROW_UNROLL, n, single, 0)


def _moe_kernel(tok_ref, ea_ref, eb_ref, nvalid_ref, nused_ref, x_hbm, rw_ref, wga_ref, wua_ref,
                wda_ref, wgb_ref, wub_ref, wdb_ref, g_ref, b_ref, out_hbm, gbuf, gsem, obuf, osem,
                pre_ref, *, tm, out_slab):
    i = pl.program_id(0)
    nused = nused_ref[0]
    slot = lax.rem(i, 2)
    other = 1 - slot
    slab_rows = tm * SUBLANES

    def gather_copy(tile, sl, r):
        dst = pl.multiple_of((sl * tm + r) * SUBLANES, SUBLANES)
        return pltpu.make_async_copy(x_hbm.at[tok_ref[tile * tm + r]],
                                     gbuf.at[pl.ds(dst, SUBLANES), :], gsem.at[sl])

    def start_gather(tile, sl):
        _for_rows(nvalid_ref[tile], lambda r: gather_copy(tile, sl, r).start())

    def wait_gather(tile, sl):
        group = pltpu.make_async_copy(gbuf.at[pl.ds(0, ROW_UNROLL * SUBLANES), :],
                                      gbuf.at[pl.ds(0, ROW_UNROLL * SUBLANES), :], gsem.at[sl])
        _for_rows(nvalid_ref[tile], lambda r: gather_copy(tile, sl, r).wait(), group.wait)

    def scatter_copy(tile, sl, r):
        tok = tok_ref[tile * tm + r]
        if out_slab:
            src = pl.multiple_of((sl * tm + r) * SUBLANES, SUBLANES)
            return pltpu.make_async_copy(obuf.at[pl.ds(src, SUBLANES), :], out_hbm.at[tok],
                                         osem.at[sl])
        return pltpu.make_async_copy(obuf.at[sl, pl.ds(r, 1), :], out_hbm.at[pl.ds(tok, 1), :],
                                     osem.at[sl])

    def start_scatter(tile, sl):
        _for_rows(nvalid_ref[tile], lambda r: scatter_copy(tile, sl, r).start())

    def wait_scatter(tile, sl):
        if out_slab:
            part = obuf.at[pl.ds(0, ROW_UNROLL * SUBLANES), :]
        else:
            part = obuf.at[sl, pl.ds(0, ROW_UNROLL), :]
        group = pltpu.make_async_copy(part, part, osem.at[sl])
        _for_rows(nvalid_ref[tile], lambda r: scatter_copy(tile, sl, r).wait(), group.wait)

    def finish_rows(r0, nrows):
        y = _layer_norm(pre_ref[r0:r0 + nrows, :], g_ref[...], b_ref[...])
        if out_slab:
            _slab_store(obuf, pl.multiple_of(other * slab_rows + r0 * SUBLANES, SUBLANES), y)
        else:
            obuf[other, r0:r0 + nrows, :] = y

    @pl.when(i == 0)
    def _():
        gbuf[...] = jnp.zeros_like(gbuf)
        pre_ref[...] = jnp.zeros_like(pre_ref)

    @pl.when((i == 0) & (nused > 0))
    def _():
        start_gather(0, 0)

    @pl.when(i + 1 < nused)
    def _():
        start_gather(i + 1, other)

    @pl.when((i >= 3) & (i <= nused))
    def _():
        wait_scatter(i - 3, other)

    @pl.when(i < nused)
    def _():
        wait_gather(i, slot)
        x = _slab_load(gbuf, pl.multiple_of(slot * slab_rows, SUBLANES), tm)
        xb = x.astype(BF16)
        half = tm // 2
        ra = rw_ref[pl.ds(ea_ref[i], 1), :]
        rb = rw_ref[pl.ds(eb_ref[i], 1), :]
        sa = jax.nn.sigmoid(jnp.sum(x * ra, axis=-1, keepdims=True))
        sb = jax.nn.sigmoid(jnp.sum(x * rb, axis=-1, keepdims=True))
        denom = sa + sb
        gt = jnp.dot(xb, wga_ref[...], preferred_element_type=F32)
        finish_rows(0, half)
        up = jnp.dot(xb, wua_ref[...], preferred_element_type=F32)
        finish_rows(half, half)
        hid = (jax.nn.silu(gt) * up).astype(BF16)
        f = (sa / denom) * jnp.dot(hid, wda_ref[...], preferred_element_type=F32)
        gt = jnp.dot(xb, wgb_ref[...], preferred_element_type=F32)
        up = jnp.dot(xb, wub_ref[...], preferred_element_type=F32)
        hid = (jax.nn.silu(gt) * up).astype(BF16)
        f = f + (sb / denom) * jnp.dot(hid, wdb_ref[...], preferred_element_type=F32)
        pre_ref[...] = DEEPNORM_ALPHA * x + f

    @pl.when(i == nused)
    def _():
        finish_rows(0, tm)

    @pl.when((i >= 1) & (i <= nused))
    def _():
        start_scatter(i - 1, other)

    @pl.when(i == nused)
    def _():
        for back in (2, 1):
            tile = i - back

            @pl.when(tile >= 0)
            def _():
                wait_scatter(tile, lax.rem(tile, 2))


def _moe(tok, tile_ea, tile_eb, nvalid, nused, x1, rw_rows, wg, wu, wd, g, b, layer, tm, out_slab):
    steps = tile_ea.shape[0]
    t = x1.shape[0]
    d = SUBLANES * LANES
    ff = wg.shape[-1]
    const = lambda shape: pl.BlockSpec(shape, lambda i, *_: (0,) * len(shape))
    up_a = pl.BlockSpec((None, None, d, ff), lambda i, tok, ea, *_: (layer, ea[i], 0, 0))
    up_b = pl.BlockSpec((None, None, d, ff), lambda i, tok, ea, eb, *_: (layer, eb[i], 0, 0))
    dn_a = pl.BlockSpec((None, None, ff, d), lambda i, tok, ea, *_: (layer, ea[i], 0, 0))
    dn_b = pl.BlockSpec((None, None, ff, d), lambda i, tok, ea, eb, *_: (layer, eb[i], 0, 0))
    if out_slab:
        out_shape = jax.ShapeDtypeStruct((t, SUBLANES, LANES), F32)
        obuf = pltpu.VMEM((2 * tm * SUBLANES, LANES), F32)
    else:
        out_shape = jax.ShapeDtypeStruct((t, d), F32)
        obuf = pltpu.VMEM((2, tm, d), F32)
    return pl.pallas_call(
        functools.partial(_moe_kernel, tm=tm, out_slab=out_slab),
        grid_spec=pltpu.PrefetchScalarGridSpec(
            num_scalar_prefetch=5,
            grid=(steps,),
            in_specs=[
                pl.BlockSpec(memory_space=pl.ANY), const((N_EXPERTS, d)),
                up_a, up_a, dn_a, up_b, up_b, dn_b,
                const((1, d)), const((1, d)),
            ],
            out_specs=pl.BlockSpec(memory_space=pl.ANY),
            scratch_shapes=[
                pltpu.VMEM((2 * tm * SUBLANES, LANES), F32), pltpu.SemaphoreType.DMA((2,)),
                obuf, pltpu.SemaphoreType.DMA((2,)),
                pltpu.VMEM((tm, d), F32),
            ],
        ),
        out_shape=out_shape,
        compiler_params=pltpu.CompilerParams(
            dimension_semantics=("arbitrary",), vmem_limit_bytes=V7X_VMEM_LIMIT),
        name=f"moe_{layer}",
    )(tok, tile_ea, tile_eb, nvalid, nused, x1, rw_rows, wg, wu, wd, wg, wu, wd, g, b)


def _route_and_experts(x1_slab, cls, rw_rows, wg, wu, wd, g, b, layer, tm, out_slab):
    t = cls.shape[1]
    nt = t // tm + N_CLASSES
    ntp = -(-(nt + 1) // 128) * 128
    pos, tile_ea, tile_eb, nvalid, nused = _positions(cls, tm, ntp)
    tok = _invert(pos.reshape(t), nt * tm)
    per_step = lambda a: a.reshape(ntp)[:nt + 1]
    return _moe(tok, per_step(tile_ea), per_step(tile_eb), per_step(nvalid), nused.reshape(128)[:1],
                x1_slab.reshape(t, SUBLANES, LANES), rw_rows, wg, wu, wd, g, b, layer, tm, out_slab)


def _forward(x, a_w_in, a_conv_w, a_w_out, b_w_in, b_ln_g, b_ln_b, b_ws, b_bs, b_w_out,
             router_w, router_bias, moe_w_gate, moe_w_up, moe_w_down,
             ln_mix_g, ln_mix_b, ln_ffn_g, ln_ffn_b, *, ts_conv, ts_sgu, tm):
    bn, s, d = x.shape
    assert d == SUBLANES * LANES
    t = bn * s
    rwt = jnp.concatenate(_split_bf16(router_w.T.astype(F32)), axis=0)
    rb = router_bias.astype(F32).reshape(N_EXPERTS, 1)
    rw_rows = router_w.T.astype(F32)
    wg = moe_w_gate.astype(BF16)
    wu = moe_w_up.astype(BF16)
    wd = moe_w_down.astype(BF16)
    row = lambda a: a.reshape(1, -1)

    x1, cls = _conv_mixer(x, a_w_in[0].astype(BF16), a_conv_w[0], a_w_out[0].astype(BF16),
                          row(ln_mix_g[0]), row(ln_mix_b[0]), rwt, rb, ts_conv)
    x2 = _route_and_experts(x1, cls, rw_rows, wg, wu, wd,
                            row(ln_ffn_g[0]), row(ln_ffn_b[0]), 0, tm, True)
    x3, cls = _sgu_mixer(x2.reshape(t * SUBLANES, LANES), b_w_in[0].astype(BF16),
                         row(b_ln_g[0]), row(b_ln_b[0]), b_ws[0], b_bs[0].T,
                         b_w_out[0].astype(BF16), row(ln_mix_g[1]), row(ln_mix_b[1]), rwt, rb,
                         ts_sgu)
    x4 = _route_and_experts(x3, cls, rw_rows, wg, wu, wd,
                            row(ln_ffn_g[1]), row(ln_ffn_b[1]), 1, tm, False)
    return x4.reshape(bn, s, d)


def kernel(x, a_w_in, a_conv_w, a_w_out, b_w_in, b_ln_g, b_ln_b, b_ws, b_bs, b_w_out, router_w, router_bias, moe_w_gate, moe_w_up, moe_w_down, ln_mix_g, ln_mix_b, ln_ffn_g, ln_ffn_b):
    return _forward(x, a_w_in, a_conv_w, a_w_out, b_w_in, b_ln_g, b_ln_b, b_ws, b_bs, b_w_out,
                    router_w, router_bias, moe_w_gate, moe_w_up, moe_w_down,
                    ln_mix_g, ln_mix_b, ln_ffn_g, ln_ffn_b,
                    ts_conv=512, ts_sgu=256, tm=256)
```

```python
import functools

import jax
import jax.numpy as jnp
from jax import lax
from jax.experimental import pallas as pl
from jax.experimental.pallas import tpu as pltpu

F32 = jnp.float32
BF16 = jnp.bfloat16

N_EXPERTS = 16
N_GROUPS = 4
EXPERTS_PER_GROUP = 4
CHUNK = 128
SGU_HEADS = 8
CONV_WIDTH = 3
DEPTH = 2
DEEPNORM_ALPHA = (2 * DEPTH) ** 0.25
LN_EPS = 1e-5

_PAIRS = ((0, 1), (0, 2), (0, 3), (1, 3), (2, 3), (2, 1))
N_PAIRS = len(_PAIRS)
N_CLASSES = N_GROUPS * N_PAIRS
CLASS_ROWS = 32
_CLASS_EA = tuple(g * EXPERTS_PER_GROUP + p[0] for g in range(N_GROUPS) for p in _PAIRS)
_CLASS_EB = tuple(g * EXPERTS_PER_GROUP + p[1] for g in range(N_GROUPS) for p in _PAIRS)

LANES = 128
SUBLANES = 8
V7X_VMEM_LIMIT = 56 * 1024 * 1024
ROW_UNROLL = 16
INVERT_UNROLL = 16


def _layer_norm(y, g, b):
    mu = jnp.mean(y, axis=-1, keepdims=True)
    d = y - mu
    var = jnp.mean(d * d, axis=-1, keepdims=True)
    return d * lax.rsqrt(var + LN_EPS) * g + b


def _gelu(x):
    return 0.5 * x * (1.0 + lax.erf(x * (2.0 ** -0.5)))


def _slab_load(ref, base, rows):
    return jnp.concatenate(
        [ref[pl.ds(base + s, rows, stride=SUBLANES), :] for s in range(SUBLANES)], axis=1)


def _slab_store(ref, base, val):
    rows = val.shape[0]
    for s in range(SUBLANES):
        ref[pl.ds(base + s, rows, stride=SUBLANES), :] = val[:, s * LANES:(s + 1) * LANES]


def _split_bf16(a):
    hi = a.astype(BF16)
    return hi, (a - hi.astype(F32)).astype(BF16)


def _route_class(x1, rw_split, rb):
    x_hi, x_lo = _split_bf16(x1)
    nt_dims = (((1,), (1,)), ((), ()))
    by_hi = lax.dot_general(rw_split, x_hi, nt_dims, preferred_element_type=F32)
    by_lo = lax.dot_general(rw_split[:N_EXPERTS, :], x_lo, nt_dims, preferred_element_type=F32)
    logits = by_hi[:N_EXPERTS, :] + (by_hi[N_EXPERTS:, :] + by_lo)
    biased = jax.nn.sigmoid(logits) + rb
    v = [biased[e:e + 1, :] for e in range(N_EXPERTS)]
    best_score = None
    best_cls = None
    for g in range(N_GROUPS):
        vg = v[g * EXPERTS_PER_GROUP:(g + 1) * EXPERTS_PER_GROUP]
        sel = []
        for i in range(EXPERTS_PER_GROUP):
            rank = jnp.zeros_like(vg[i], dtype=jnp.int32)
            for j in range(EXPERTS_PER_GROUP):
                if j == i:
                    continue
                beats = (vg[j] >= vg[i]) if j < i else (vg[j] > vg[i])
                rank = rank + beats.astype(jnp.int32)
            sel.append(rank < 2)
        top_sum = None
        cls_g = jnp.zeros_like(vg[0], dtype=jnp.int32)
        for p, (a, b) in enumerate(_PAIRS):
            is_pair = sel[a] & sel[b]
            cls_g = jnp.where(is_pair, g * N_PAIRS + p, cls_g)
            pair_sum = vg[min(a, b)] + vg[max(a, b)]
            top_sum = jnp.where(is_pair, pair_sum, 0.0 if top_sum is None else top_sum)
        if best_score is None:
            best_score, best_cls = top_sum, cls_g
        else:
            better = top_sum > best_score
            best_score = jnp.where(better, top_sum, best_score)
            best_cls = jnp.where(better, cls_g, best_cls)
    return best_cls


def _finish_rows(pre_ref, g_ref, b_ref, rwt_ref, rb_ref, x1_ref, cls_ref, r0, nrows):
    x1 = _layer_norm(pre_ref[r0:r0 + nrows, :], g_ref[...], b_ref[...])
    _slab_store(x1_ref, r0 * SUBLANES, x1)
    cls_ref[:, r0:r0 + nrows] = _route_class(x1, rwt_ref[...], rb_ref[...])


def _conv_mixer_kernel(x_ref, w_in_ref, cw_ref, w_out_ref, g_ref, b_ref, rwt_ref, rb_ref,
                       x1_ref, cls_ref, carry_ref, pre_ref, *, tiles_per_seq):
    ts, d = x_ref.shape
    s = pl.program_id(0)

    @pl.when(s == 0)
    def _():
        pre_ref[...] = jnp.zeros_like(pre_ref)

    @pl.when(lax.rem(s, tiles_per_seq) == 0)
    def _():
        carry_ref[...] = jnp.zeros_like(carry_ref)

    x = x_ref[...]
    xb = x.astype(BF16)
    n_chunks = 2
    piece = 2 * d // n_chunks
    rows_per = ts // n_chunks
    parts = []
    for k in range(n_chunks):
        parts.append(jnp.dot(xb, w_in_ref[:, d + k * piece:d + (k + 1) * piece],
                             preferred_element_type=F32))
        _finish_rows(pre_ref, g_ref, b_ref, rwt_ref, rb_ref, x1_ref, cls_ref,
                     k * rows_per, rows_per)
    gate_h = jnp.concatenate(parts, axis=1)
    ch = gate_h[:, 0:d] * gate_h[:, d:2 * d]
    prev2 = carry_ref[6:7, :]
    prev1 = carry_ref[7:8, :]
    rows = lax.broadcasted_iota(jnp.int32, (SUBLANES, 1), 0)
    ch1 = pltpu.roll(ch, 1, 0)
    ch2 = pltpu.roll(ch, 2, 0)
    head1 = jnp.where(rows == 0, prev1, ch1[0:SUBLANES, :])
    head2 = jnp.where(rows == 0, prev2, jnp.where(rows == 1, prev1, ch2[0:SUBLANES, :]))
    ch1 = jnp.concatenate([head1, ch1[SUBLANES:, :]], axis=0)
    ch2 = jnp.concatenate([head2, ch2[SUBLANES:, :]], axis=0)
    z = cw_ref[0:1, :] * ch2 + cw_ref[1:2, :] * ch1 + cw_ref[2:3, :] * ch
    carry_ref[...] = ch[ts - 8:ts, :]
    bg = jnp.dot(xb, w_in_ref[:, 0:d], preferred_element_type=F32)
    m = jnp.dot((bg * z).astype(BF16), w_out_ref[...], preferred_element_type=F32)
    pre_ref[...] = DEEPNORM_ALPHA * x + m


def _conv_mixer(x, w_in, conv_w, w_out, ln_g, ln_b, rwt, rb, ts):
    bn, s, d = x.shape
    ns = s // ts
    n = bn * ns
    const = lambda shape: pl.BlockSpec(shape, lambda i: (0,) * len(shape))
    cur = lambda i: jnp.minimum(i, n - 1)
    prev = lambda i: jnp.maximum(i - 1, 0)
    return pl.pallas_call(
        functools.partial(_conv_mixer_kernel, tiles_per_seq=ns),
        grid=(n + 1,),
        in_specs=[
            pl.BlockSpec((None, ts, d), lambda i: (cur(i) // ns, cur(i) % ns, 0)),
            const((d, 3 * d)), const((CONV_WIDTH, d)), const((d, d)),
            const((1, d)), const((1, d)), const((2 * N_EXPERTS, d)), const((N_EXPERTS, 1)),
        ],
        out_specs=[
            pl.BlockSpec((ts * SUBLANES, LANES), lambda i: (prev(i), 0)),
            pl.BlockSpec((1, ts), lambda i: (0, prev(i))),
        ],
        out_shape=[
            jax.ShapeDtypeStruct((bn * s * SUBLANES, LANES), F32),
            jax.ShapeDtypeStruct((1, bn * s), jnp.int32),
        ],
        scratch_shapes=[pltpu.VMEM((8, d), F32), pltpu.VMEM((ts, d), F32)],
        compiler_params=pltpu.CompilerParams(
            dimension_semantics=("arbitrary",), vmem_limit_bytes=V7X_VMEM_LIMIT),
        name="conv_mixer",
    )(x, w_in, conv_w, w_out, ln_g, ln_b, rwt, rb)


def _sgu_mixer_kernel(xs_ref, w_in_ref, lng_ref, lnb_ref, ws_ref, bst_ref, w_out_ref,
                      g_ref, b_ref, rwt_ref, rb_ref, x1_ref, cls_ref, v_ref, gate_ref, pre_ref):
    ts, width = v_ref.shape
    hd = width // SGU_HEADS

    @pl.when(pl.program_id(0) == 0)
    def _():
        pre_ref[...] = jnp.zeros_like(pre_ref)

    x = _slab_load(xs_ref, 0, ts)
    xb = x.astype(BF16)
    half = width // 2
    v_lo = jnp.dot(xb, w_in_ref[:, width:width + half], preferred_element_type=F32)
    _finish_rows(pre_ref, g_ref, b_ref, rwt_ref, rb_ref, x1_ref, cls_ref, 0, ts)
    v_hi = jnp.dot(xb, w_in_ref[:, width + half:2 * width], preferred_element_type=F32)
    v = _gelu(jnp.concatenate([v_lo, v_hi], axis=1))
    v_ref[...] = _layer_norm(v, lng_ref[...], lnb_ref[...]).astype(BF16)
    r_i = lax.broadcasted_iota(jnp.int32, (CHUNK, CHUNK), 0)
    c_i = lax.broadcasted_iota(jnp.int32, (CHUNK, CHUNK), 1)
    causal = r_i >= c_i
    for h in range(SGU_HEADS):
        cols = slice(h * hd, (h + 1) * hd)
        u_h = _gelu(jnp.dot(xb, w_in_ref[:, cols], preferred_element_type=F32))
        w_h = jnp.where(causal, ws_ref[h], 0.0).astype(BF16)
        bias = bst_ref[:, h:h + 1]
        for c in range(ts // CHUNK):
            rws = slice(c * CHUNK, (c + 1) * CHUNK)
            mixed = jnp.dot(w_h, v_ref[rws, cols], preferred_element_type=F32) + bias
            gate_ref[rws, cols] = (u_h[rws, :] * mixed).astype(BF16)
    m = jnp.dot(gate_ref[...], w_out_ref[...], preferred_element_type=F32)
    pre_ref[...] = DEEPNORM_ALPHA * x + m


def _sgu_mixer(xs, w_in, ln_g, ln_b, ws, bst, w_out, g, b, rwt, rb, ts):
    t = xs.shape[0] // SUBLANES
    n = t // ts
    width, d = w_out.shape
    const = lambda shape: pl.BlockSpec(shape, lambda i: (0,) * len(shape))
    cur = lambda i: jnp.minimum(i, n - 1)
    prev = lambda i: jnp.maximum(i - 1, 0)
    return pl.pallas_call(
        _sgu_mixer_kernel,
        grid=(n + 1,),
        in_specs=[
            pl.BlockSpec((ts * SUBLANES, LANES), lambda i: (cur(i), 0)),
            const((d, 2 * width)), const((1, width)), const((1, width)),
            const((SGU_HEADS, CHUNK, CHUNK)), const((CHUNK, SGU_HEADS)), const((width, d)),
            const((1, d)), const((1, d)), const((2 * N_EXPERTS, d)), const((N_EXPERTS, 1)),
        ],
        out_specs=[
            pl.BlockSpec((ts * SUBLANES, LANES), lambda i: (prev(i), 0)),
            pl.BlockSpec((1, ts), lambda i: (0, prev(i))),
        ],
        out_shape=[
            jax.ShapeDtypeStruct((t * SUBLANES, LANES), F32),
            jax.ShapeDtypeStruct((1, t), jnp.int32),
        ],
        scratch_shapes=[pltpu.VMEM((ts, width), BF16), pltpu.VMEM((ts, width), BF16),
                        pltpu.VMEM((ts, d), F32)],
        compiler_params=pltpu.CompilerParams(
            dimension_semantics=("arbitrary",), vmem_limit_bytes=V7X_VMEM_LIMIT),
        name="sgu_mixer",
    )(xs, w_in, ln_g, ln_b, ws, bst, w_out, g, b, rwt, rb)


def _positions_kernel(cls_ref, pos_ref, tile_ea_ref, tile_eb_ref, nvalid_ref, nused_ref,
                      incl_ref, *, tm):
    t = cls_ref.shape[1]
    ntp = tile_ea_ref.shape[1]
    lanes = 256
    cls = cls_ref[...]
    crow = lax.broadcasted_iota(jnp.int32, (CLASS_ROWS, t), 0)
    onehot = (crow == cls).astype(F32)
    k_i = lax.broadcasted_iota(jnp.int32, (lanes, lanes), 0)
    j_i = lax.broadcasted_iota(jnp.int32, (lanes, lanes), 1)
    upper = (k_i <= j_i).astype(BF16)
    count = jnp.zeros((CLASS_ROWS, 1), F32)
    for c in range(t // lanes):
        sl = slice(c * lanes, (c + 1) * lanes)
        inc = jnp.dot(onehot[:, sl].astype(BF16), upper, preferred_element_type=F32) + count
        incl_ref[:, sl] = inc
        count = inc[:, lanes - 1:lanes]
    rank = jnp.sum(onehot * incl_ref[...], axis=0, keepdims=True) - 1.0
    ntile = jnp.floor((count + (tm - 1)) * (1.0 / tm))
    srow = lax.broadcasted_iota(jnp.int32, (CLASS_ROWS, 1), 0)
    tstart = jnp.zeros((CLASS_ROWS, 1), F32)
    for c in range(N_CLASSES):
        tstart = tstart + jnp.where(srow > c, ntile[c:c + 1, :], 0.0)
    tend = tstart + ntile
    pos = jnp.sum(onehot * (tstart * tm), axis=0, keepdims=True) + rank
    pos_ref[...] = pos.astype(jnp.int32)
    nused = jnp.max(jnp.where(srow < N_CLASSES, tend, 0.0), axis=0, keepdims=True)
    last_cls = jnp.max(jnp.where((ntile > 0) & (srow < N_CLASSES), srow, 0), axis=0, keepdims=True)
    tile_i = lax.broadcasted_iota(jnp.int32, (CLASS_ROWS, ntp), 1).astype(F32)
    is_cls = lax.broadcasted_iota(jnp.int32, (CLASS_ROWS, ntp), 0) < N_CLASSES
    done = (tile_i >= tend) & is_cls
    tcls = jnp.minimum(jnp.sum(done.astype(jnp.int32), axis=0, keepdims=True), last_cls)
    ea = jnp.zeros((1, ntp), jnp.int32)
    eb = jnp.zeros((1, ntp), jnp.int32)
    for c in range(N_CLASSES):
        ea = jnp.where(tcls == c, _CLASS_EA[c], ea)
        eb = jnp.where(tcls == c, _CLASS_EB[c], eb)
    tile_ea_ref[...] = ea
    tile_eb_ref[...] = eb
    inside = (tile_i >= tstart) & (tile_i < tend) & is_cls
    left = jnp.minimum(count - (tile_i - tstart) * tm, float(tm))
    nvalid_ref[...] = jnp.sum(jnp.where(inside, left, 0.0), axis=0, keepdims=True).astype(jnp.int32)
    nused_ref[...] = jnp.broadcast_to(nused, nused_ref.shape).astype(jnp.int32)


def _positions(cls, tm, ntp):
    t = cls.shape[1]
    return pl.pallas_call(
        functools.partial(_positions_kernel, tm=tm),
        out_shape=[
            jax.ShapeDtypeStruct((1, t), jnp.int32),
            jax.ShapeDtypeStruct((1, ntp), jnp.int32),
            jax.ShapeDtypeStruct((1, ntp), jnp.int32),
            jax.ShapeDtypeStruct((1, ntp), jnp.int32),
            jax.ShapeDtypeStruct((1, 128), jnp.int32),
        ],
        scratch_shapes=[pltpu.VMEM((CLASS_ROWS, t), F32)],
        compiler_params=pltpu.CompilerParams(vmem_limit_bytes=V7X_VMEM_LIMIT),
        name="positions",
    )(cls)


def _invert_kernel(pos_ref, tok_ref):
    t = pos_ref.shape[0]
    r = tok_ref.shape[0]

    def fill(j, carry):
        for u in range(INVERT_UNROLL):
            tok_ref[j * INVERT_UNROLL + u] = 0
        return carry

    lax.fori_loop(0, r // INVERT_UNROLL, fill, 0)

    def put(j, carry):
        k0 = j * INVERT_UNROLL
        dst = [pos_ref[k0 + u] for u in range(INVERT_UNROLL)]
        for u in range(INVERT_UNROLL):
            tok_ref[dst[u]] = k0 + u
        return carry

    lax.fori_loop(0, t // INVERT_UNROLL, put, 0)


def _invert(pos, r):
    return pl.pallas_call(
        _invert_kernel,
        in_specs=[pl.BlockSpec(memory_space=pltpu.SMEM)],
        out_specs=pl.BlockSpec(memory_space=pltpu.SMEM),
        out_shape=jax.ShapeDtypeStruct((r,), jnp.int32),
        name="invert_positions",
    )(pos)


def _for_rows(n, per_row, per_group=None):
    ngroups = n // ROW_UNROLL

    def group(j, carry):
        if per_group is not None:
            per_group()
        else:
            for u in range(ROW_UNROLL):
                per_row(j * ROW_UNROLL + u)
        return carry

    lax.fori_loop(0, ngroups, group, 0)

    def single(r, carry):
        per_row(r)
        return carry

    lax.fori_loop(ngroups * ROW_UNROLL, n, single, 0)


def _moe_kernel(tok_ref, ea_ref, eb_ref, nvalid_ref, nused_ref, x_hbm, rw_ref, wga_ref, wua_ref,
                wda_ref, wgb_ref, wub_ref, wdb_ref, g_ref, b_ref, out_hbm, gbuf, gsem, obuf, osem,
                pre_ref, *, tm, out_slab):
    i = pl.program_id(0)
    nused = nused_ref[0]
    slot = lax.rem(i, 2)
    other = 1 - slot
    slab_rows = tm * SUBLANES

    def gather_copy(tile, sl, r):
        dst = pl.multiple_of((sl * tm + r) * SUBLANES, SUBLANES)
        return pltpu.make_async_copy(x_hbm.at[tok_ref[tile * tm + r]],
                                     gbuf.at[pl.ds(dst, SUBLANES), :], gsem.at[sl])

    def start_gather(tile, sl):
        _for_rows(nvalid_ref[tile], lambda r: gather_copy(tile, sl, r).start())

    def wait_gather(tile, sl):
        group = pltpu.make_async_copy(gbuf.at[pl.ds(0, ROW_UNROLL * SUBLANES), :],
                                      gbuf.at[pl.ds(0, ROW_UNROLL * SUBLANES), :], gsem.at[sl])
        _for_rows(nvalid_ref[tile], lambda r: gather_copy(tile, sl, r).wait(), group.wait)

    def scatter_copy(tile, sl, r):
        tok = tok_ref[tile * tm + r]
        if out_slab:
            src = pl.multiple_of((sl * tm + r) * SUBLANES, SUBLANES)
            return pltpu.make_async_copy(obuf.at[pl.ds(src, SUBLANES), :], out_hbm.at[tok],
                                         osem.at[sl])
        return pltpu.make_async_copy(obuf.at[sl, pl.ds(r, 1), :], out_hbm.at[pl.ds(tok, 1), :],
                                     osem.at[sl])

    def start_scatter(tile, sl):
        _for_rows(nvalid_ref[tile], lambda r: scatter_copy(tile, sl, r).start())

    def wait_scatter(tile, sl):
        if out_slab:
            part = obuf.at[pl.ds(0, ROW_UNROLL * SUBLANES), :]
        else:
            part = obuf.at[sl, pl.ds(0, ROW_UNROLL), :]
        group = pltpu.make_async_copy(part, part, osem.at[sl])
        _for_rows(nvalid_ref[tile], lambda r: scatter_copy(tile, sl, r).wait(), group.wait)

    def finish_rows(r0, nrows):
        y = _layer_norm(pre_ref[r0:r0 + nrows, :], g_ref[...], b_ref[...])
        if out_slab:
            _slab_store(obuf, pl.multiple_of(other * slab_rows + r0 * SUBLANES, SUBLANES), y)
        else:
            obuf[other, r0:r0 + nrows, :] = y

    @pl.when(i == 0)
    def _():
        gbuf[...] = jnp.zeros_like(gbuf)
        pre_ref[...] = jnp.zeros_like(pre_ref)

    @pl.when((i == 0) & (nused > 0))
    def _():
        start_gather(0, 0)

    @pl.when(i + 1 < nused)
    def _():
        start_gather(i + 1, other)

    @pl.when((i >= 3) & (i <= nused))
    def _():
        wait_scatter(i - 3, other)

    @pl.when(i < nused)
    def _():
        wait_gather(i, slot)
        x = _slab_load(gbuf, pl.multiple_of(slot * slab_rows, SUBLANES), tm)
        xb = x.astype(BF16)
        half = tm // 2
        ra = rw_ref[pl.ds(ea_ref[i], 1), :]
        rb = rw_ref[pl.ds(eb_ref[i], 1), :]
        sa = jax.nn.sigmoid(jnp.sum(x * ra, axis=-1, keepdims=True))
        sb = jax.nn.sigmoid(jnp.sum(x * rb, axis=-1, keepdims=True))
        denom = sa + sb
        gt = jnp.dot(xb, wga_ref[...].astype(BF16), preferred_element_type=F32)
        finish_rows(0, half)
        up = jnp.dot(xb, wua_ref[...].astype(BF16), preferred_element_type=F32)
        finish_rows(half, half)
        hid = (jax.nn.silu(gt) * up).astype(BF16)
        f = (sa / denom) * jnp.dot(hid, wda_ref[...].astype(BF16), preferred_element_type=F32)
        gt = jnp.dot(xb, wgb_ref[...].astype(BF16), preferred_element_type=F32)
        up = jnp.dot(xb, wub_ref[...].astype(BF16), preferred_element_type=F32)
        hid = (jax.nn.silu(gt) * up).astype(BF16)
        f = f + (sb / denom) * jnp.dot(hid, wdb_ref[...].astype(BF16), preferred_element_type=F32)
        pre_ref[...] = DEEPNORM_ALPHA * x + f

    @pl.when(i == nused)
    def _():
        finish_rows(0, tm)

    @pl.when((i >= 1) & (i <= nused))
    def _():
        start_scatter(i - 1, other)

    @pl.when(i == nused)
    def _():
        for back in (2, 1):
            tile = i - back

            @pl.when(tile >= 0)
            def _():
                wait_scatter(tile, lax.rem(tile, 2))


def _moe(tok, tile_ea, tile_eb, nvalid, nused, x1, rw_rows, wg, wu, wd, g, b, layer, tm, out_slab):
    steps = tile_ea.shape[0]
    t = x1.shape[0]
    d = SUBLANES * LANES
    ff = wg.shape[-1]
    const = lambda shape: pl.BlockSpec(shape, lambda i, *_: (0,) * len(shape))
    up_a = pl.BlockSpec((None, None, d, ff), lambda i, tok, ea, *_: (layer, ea[i], 0, 0))
    up_b = pl.BlockSpec((None, None, d, ff), lambda i, tok, ea, eb, *_: (layer, eb[i], 0, 0))
    dn_a = pl.BlockSpec((None, None, ff, d), lambda i, tok, ea, *_: (layer, ea[i], 0, 0))
    dn_b = pl.BlockSpec((None, None, ff, d), lambda i, tok, ea, eb, *_: (layer, eb[i], 0, 0))
    if out_slab:
        out_shape = jax.ShapeDtypeStruct((t, SUBLANES, LANES), F32)
        obuf = pltpu.VMEM((2 * tm * SUBLANES, LANES), F32)
    else:
        out_shape = jax.ShapeDtypeStruct((t, d), F32)
        obuf = pltpu.VMEM((2, tm, d), F32)
    return pl.pallas_call(
        functools.partial(_moe_kernel, tm=tm, out_slab=out_slab),
        grid_spec=pltpu.PrefetchScalarGridSpec(
            num_scalar_prefetch=5,
            grid=(steps,),
            in_specs=[
                pl.BlockSpec(memory_space=pl.ANY), const((N_EXPERTS, d)),
                up_a, up_a, dn_a, up_b, up_b, dn_b,
                const((1, d)), const((1, d)),
            ],
            out_specs=pl.BlockSpec(memory_space=pl.ANY),
            scratch_shapes=[
                pltpu.VMEM((2 * tm * SUBLANES, LANES), F32), pltpu.SemaphoreType.DMA((2,)),
                obuf, pltpu.SemaphoreType.DMA((2,)),
                pltpu.VMEM((tm, d), F32),
            ],
        ),
        out_shape=out_shape,
        compiler_params=pltpu.CompilerParams(
            dimension_semantics=("arbitrary",), vmem_limit_bytes=V7X_VMEM_LIMIT),
        name=f"moe_{layer}",
    )(tok, tile_ea, tile_eb, nvalid, nused, x1, rw_rows, wg, wu, wd, wg, wu, wd, g, b)


def _route_and_experts(x1_slab, cls, rw_rows, wg, wu, wd, g, b, layer, tm, out_slab):
    t = cls.shape[1]
    nt = t // tm + N_CLASSES
    ntp = -(-(nt + 1) // 128) * 128
    pos, tile_ea, tile_eb, nvalid, nused = _positions(cls, tm, ntp)
    tok = _invert(pos.reshape(t), nt * tm)
    per_step = lambda a: a.reshape(ntp)[:nt + 1]
    return _moe(tok, per_step(tile_ea), per_step(tile_eb), per_step(nvalid), nused.reshape(128)[:1],
                x1_slab.reshape(t, SUBLANES, LANES), rw_rows, wg, wu, wd, g, b, layer, tm, out_slab)


def _forward(x, a_w_in, a_conv_w, a_w_out, b_w_in, b_ln_g, b_ln_b, b_ws, b_bs, b_w_out,
             router_w, router_bias, moe_w_gate, moe_w_up, moe_w_down,
             ln_mix_g, ln_mix_b, ln_ffn_g, ln_ffn_b, *, ts_conv, ts_sgu, tm):
    bn, s, d = x.shape
    assert d == SUBLANES * LANES
    t = bn * s
    rwt = jnp.concatenate(_split_bf16(router_w.T.astype(F32)), axis=0)
    rb = router_bias.astype(F32).reshape(N_EXPERTS, 1)
    rw_rows = router_w.T.astype(F32)
    wg, wu, wd = moe_w_gate, moe_w_up, moe_w_down
    row = lambda a: a.reshape(1, -1)

    x1, cls = _conv_mixer(x, a_w_in[0].astype(BF16), a_conv_w[0], a_w_out[0].astype(BF16),
                          row(ln_mix_g[0]), row(ln_mix_b[0]), rwt, rb, ts_conv)
    x2 = _route_and_experts(x1, cls, rw_rows, wg, wu, wd,
                            row(ln_ffn_g[0]), row(ln_ffn_b[0]), 0, tm, True)
    x3, cls = _sgu_mixer(x2.reshape(t * SUBLANES, LANES), b_w_in[0].astype(BF16),
                         row(b_ln_g[0]), row(b_ln_b[0]), b_ws[0], b_bs[0].T,
                         b_w_out[0].astype(BF16), row(ln_mix_g[1]), row(ln_mix_b[1]), rwt, rb,
                         ts_sgu)
    x4 = _route_and_experts(x3, cls, rw_rows, wg, wu, wd,
                            row(ln_ffn_g[1]), row(ln_ffn_b[1]), 1, tm, False)
    return x4.reshape(bn, s, d)


def kernel(x, a_w_in, a_conv_w, a_w_out, b_w_in, b_ln_g, b_ln_b, b_ws, b_bs, b_w_out, router_w, router_bias, moe_w_gate, moe_w_up, moe_w_down, ln_mix_g, ln_mix_b, ln_ffn_g, ln_ffn_b):
    return _forward(x, a_w_in, a_conv_w, a_w_out, b_w_in, b_ln_g, b_ln_b, b_ws, b_bs, b_w_out,
                    router_w, router_bias, moe_w_gate, moe_w_up, moe_w_down,
                    ln_mix_g, ln_mix_b, ln_ffn_g, ln_ffn_b,
                    ts_conv=512, ts_sgu=256, tm=256)
```

```python
import functools

import jax
import jax.numpy as jnp
from jax import lax
from jax.experimental import pallas as pl
from jax.experimental.pallas import tpu as pltpu

F32 = jnp.float32
BF16 = jnp.bfloat16

N_EXPERTS = 16
N_GROUPS = 4
EXPERTS_PER_GROUP = 4
CHUNK = 128
SGU_HEADS = 8
CONV_WIDTH = 3
DEPTH = 2
DEEPNORM_ALPHA = (2 * DEPTH) ** 0.25
LN_EPS = 1e-5

_PAIRS = ((0, 1), (0, 2), (0, 3), (1, 3), (2, 3), (2, 1))
N_PAIRS = len(_PAIRS)
N_CLASSES = N_GROUPS * N_PAIRS
CLASS_ROWS = 32
_CLASS_EA = tuple(g * EXPERTS_PER_GROUP + p[0] for g in range(N_GROUPS) for p in _PAIRS)
_CLASS_EB = tuple(g * EXPERTS_PER_GROUP + p[1] for g in range(N_GROUPS) for p in _PAIRS)

LANES = 128
SUBLANES = 8
V7X_VMEM_LIMIT = 56 * 1024 * 1024
ROW_UNROLL = 16
INVERT_UNROLL = 16


def _layer_norm(y, g, b):
    mu = jnp.mean(y, axis=-1, keepdims=True)
    d = y - mu
    var = jnp.mean(d * d, axis=-1, keepdims=True)
    return d * lax.rsqrt(var + LN_EPS) * g + b


def _gelu(x):
    return 0.5 * x * (1.0 + lax.erf(x * (2.0 ** -0.5)))


def _slab_load(ref, base, rows):
    return jnp.concatenate(
        [ref[pl.ds(base + s, rows, stride=SUBLANES), :] for s in range(SUBLANES)], axis=1)


def _slab_store(ref, base, val):
    rows = val.shape[0]
    for s in range(SUBLANES):
        ref[pl.ds(base + s, rows, stride=SUBLANES), :] = val[:, s * LANES:(s + 1) * LANES]


def _split_bf16(a):
    hi = a.astype(BF16)
    return hi, (a - hi.astype(F32)).astype(BF16)


def _route_class(x1, rw_split, rb):
    x_hi, x_lo = _split_bf16(x1)
    nt_dims = (((1,), (1,)), ((), ()))
    by_hi = lax.dot_general(rw_split, x_hi, nt_dims, preferred_element_type=F32)
    by_lo = lax.dot_general(rw_split[:N_EXPERTS, :], x_lo, nt_dims, preferred_element_type=F32)
    logits = by_hi[:N_EXPERTS, :] + (by_hi[N_EXPERTS:, :] + by_lo)
    biased = jax.nn.sigmoid(logits) + rb
    v = [biased[e:e + 1, :] for e in range(N_EXPERTS)]
    best_score = None
    best_cls = None
    for g in range(N_GROUPS):
        vg = v[g * EXPERTS_PER_GROUP:(g + 1) * EXPERTS_PER_GROUP]
        sel = []
        for i in range(EXPERTS_PER_GROUP):
            rank = jnp.zeros_like(vg[i], dtype=jnp.int32)
            for j in range(EXPERTS_PER_GROUP):
                if j == i:
                    continue
                beats = (vg[j] >= vg[i]) if j < i else (vg[j] > vg[i])
                rank = rank + beats.astype(jnp.int32)
            sel.append(rank < 2)
        top_sum = None
        cls_g = jnp.zeros_like(vg[0], dtype=jnp.int32)
        for p, (a, b) in enumerate(_PAIRS):
            is_pair = sel[a] & sel[b]
            cls_g = jnp.where(is_pair, g * N_PAIRS + p, cls_g)
            pair_sum = vg[min(a, b)] + vg[max(a, b)]
            top_sum = jnp.where(is_pair, pair_sum, 0.0 if top_sum is None else top_sum)
        if best_score is None:
            best_score, best_cls = top_sum, cls_g
        else:
            better = top_sum > best_score
            best_score = jnp.where(better, top_sum, best_score)
            best_cls = jnp.where(better, cls_g, best_cls)
    return best_cls


def _finish_rows(pre_ref, g_ref, b_ref, rwt_ref, rb_ref, x1_ref, cls_ref, r0, nrows):
    x1 = _layer_norm(pre_ref[r0:r0 + nrows, :], g_ref[...], b_ref[...])
    _slab_store(x1_ref, r0 * SUBLANES, x1)
    cls_ref[:, r0:r0 + nrows] = _route_class(x1, rwt_ref[...], rb_ref[...])


def _conv_mixer_kernel(x_ref, w_in_ref, cw_ref, w_out_ref, g_ref, b_ref, rwt_ref, rb_ref,
                       x1_ref, cls_ref, carry_ref, pre_ref, *, tiles_per_seq):
    ts, d = x_ref.shape
    s = pl.program_id(0)

    @pl.when(s == 0)
    def _():
        pre_ref[...] = jnp.zeros_like(pre_ref)

    @pl.when(lax.rem(s, tiles_per_seq) == 0)
    def _():
        carry_ref[...] = jnp.zeros_like(carry_ref)

    x = x_ref[...]
    xb = x.astype(BF16)
    n_chunks = 2
    piece = 2 * d // n_chunks
    rows_per = ts // n_chunks
    parts = []
    for k in range(n_chunks):
        parts.append(jnp.dot(xb, w_in_ref[:, d + k * piece:d + (k + 1) * piece],
                             preferred_element_type=F32))
        _finish_rows(pre_ref, g_ref, b_ref, rwt_ref, rb_ref, x1_ref, cls_ref,
                     k * rows_per, rows_per)
    gate_h = jnp.concatenate(parts, axis=1)
    ch = gate_h[:, 0:d] * gate_h[:, d:2 * d]
    prev2 = carry_ref[6:7, :]
    prev1 = carry_ref[7:8, :]
    rows = lax.broadcasted_iota(jnp.int32, (SUBLANES, 1), 0)
    ch1 = pltpu.roll(ch, 1, 0)
    ch2 = pltpu.roll(ch, 2, 0)
    head1 = jnp.where(rows == 0, prev1, ch1[0:SUBLANES, :])
    head2 = jnp.where(rows == 0, prev2, jnp.where(rows == 1, prev1, ch2[0:SUBLANES, :]))
    ch1 = jnp.concatenate([head1, ch1[SUBLANES:, :]], axis=0)
    ch2 = jnp.concatenate([head2, ch2[SUBLANES:, :]], axis=0)
    z = cw_ref[0:1, :] * ch2 + cw_ref[1:2, :] * ch1 + cw_ref[2:3, :] * ch
    carry_ref[...] = ch[ts - 8:ts, :]
    bg = jnp.dot(xb, w_in_ref[:, 0:d], preferred_element_type=F32)
    m = jnp.dot((bg * z).astype(BF16), w_out_ref[...], preferred_element_type=F32)
    pre_ref[...] = DEEPNORM_ALPHA * x + m


def _conv_mixer(x, w_in, conv_w, w_out, ln_g, ln_b, rwt, rb, ts):
    bn, s, d = x.shape
    ns = s // ts
    n = bn * ns
    const = lambda shape: pl.BlockSpec(shape, lambda i: (0,) * len(shape))
    cur = lambda i: jnp.minimum(i, n - 1)
    prev = lambda i: jnp.maximum(i - 1, 0)
    return pl.pallas_call(
        functools.partial(_conv_mixer_kernel, tiles_per_seq=ns),
        grid=(n + 1,),
        in_specs=[
            pl.BlockSpec((None, ts, d), lambda i: (cur(i) // ns, cur(i) % ns, 0)),
            const((d, 3 * d)), const((CONV_WIDTH, d)), const((d, d)),
            const((1, d)), const((1, d)), const((2 * N_EXPERTS, d)), const((N_EXPERTS, 1)),
        ],
        out_specs=[
            pl.BlockSpec((ts * SUBLANES, LANES), lambda i: (prev(i), 0)),
            pl.BlockSpec((1, ts), lambda i: (0, prev(i))),
        ],
        out_shape=[
            jax.ShapeDtypeStruct((bn * s * SUBLANES, LANES), F32),
            jax.ShapeDtypeStruct((1, bn * s), jnp.int32),
        ],
        scratch_shapes=[pltpu.VMEM((8, d), F32), pltpu.VMEM((ts, d), F32)],
        compiler_params=pltpu.CompilerParams(
            dimension_semantics=("arbitrary",), vmem_limit_bytes=V7X_VMEM_LIMIT),
        name="conv_mixer",
    )(x, w_in, conv_w, w_out, ln_g, ln_b, rwt, rb)


def _sgu_mixer_kernel(xs_ref, w_in_ref, lng_ref, lnb_ref, ws_ref, bst_ref, w_out_ref,
                      g_ref, b_ref, rwt_ref, rb_ref, x1_ref, cls_ref, v_ref, gate_ref, pre_ref):
    ts, width = v_ref.shape
    hd = width // SGU_HEADS

    @pl.when(pl.program_id(0) == 0)
    def _():
        pre_ref[...] = jnp.zeros_like(pre_ref)

    x = _slab_load(xs_ref, 0, ts)
    xb = x.astype(BF16)
    half = width // 2
    v_lo = jnp.dot(xb, w_in_ref[:, width:width + half], preferred_element_type=F32)
    _finish_rows(pre_ref, g_ref, b_ref, rwt_ref, rb_ref, x1_ref, cls_ref, 0, ts)
    v_hi = jnp.dot(xb, w_in_ref[:, width + half:2 * width], preferred_element_type=F32)
    v = _gelu(jnp.concatenate([v_lo, v_hi], axis=1))
    v_ref[...] = _layer_norm(v, lng_ref[...], lnb_ref[...]).astype(BF16)
    r_i = lax.broadcasted_iota(jnp.int32, (CHUNK, CHUNK), 0)
    c_i = lax.broadcasted_iota(jnp.int32, (CHUNK, CHUNK), 1)
    causal = r_i >= c_i
    for h in range(SGU_HEADS):
        cols = slice(h * hd, (h + 1) * hd)
        u_h = _gelu(jnp.dot(xb, w_in_ref[:, cols], preferred_element_type=F32))
        w_h = jnp.where(causal, ws_ref[h], 0.0).astype(BF16)
        bias = bst_ref[:, h:h + 1]
        for c in range(ts // CHUNK):
            rws = slice(c * CHUNK, (c + 1) * CHUNK)
            mixed = jnp.dot(w_h, v_ref[rws, cols], preferred_element_type=F32) + bias
            gate_ref[rws, cols] = (u_h[rws, :] * mixed).astype(BF16)
    m = jnp.dot(gate_ref[...], w_out_ref[...], preferred_element_type=F32)
    pre_ref[...] = DEEPNORM_ALPHA * x + m


def _sgu_mixer(xs, w_in, ln_g, ln_b, ws, bst, w_out, g, b, rwt, rb, ts):
    t = xs.shape[0] // SUBLANES
    n = t // ts
    width, d = w_out.shape
    const = lambda shape: pl.BlockSpec(shape, lambda i: (0,) * len(shape))
    cur = lambda i: jnp.minimum(i, n - 1)
    prev = lambda i: jnp.maximum(i - 1, 0)
    return pl.pallas_call(
        _sgu_mixer_kernel,
        grid=(n + 1,),
        in_specs=[
            pl.BlockSpec((ts * SUBLANES, LANES), lambda i: (cur(i), 0)),
            const((d, 2 * width)), const((1, width)), const((1, width)),
            const((SGU_HEADS, CHUNK, CHUNK)), const((CHUNK, SGU_HEADS)), const((width, d)),
            const((1, d)), const((1, d)), const((2 * N_EXPERTS, d)), const((N_EXPERTS, 1)),
        ],
        out_specs=[
            pl.BlockSpec((ts * SUBLANES, LANES), lambda i: (prev(i), 0)),
            pl.BlockSpec((1, ts), lambda i: (0, prev(i))),
        ],
        out_shape=[
            jax.ShapeDtypeStruct((t * SUBLANES, LANES), F32),
            jax.ShapeDtypeStruct((1, t), jnp.int32),
        ],
        scratch_shapes=[pltpu.VMEM((ts, width), BF16), pltpu.VMEM((ts, width), BF16),
                        pltpu.VMEM((ts, d), F32)],
        compiler_params=pltpu.CompilerParams(
            dimension_semantics=("arbitrary",), vmem_limit_bytes=V7X_VMEM_LIMIT),
        name="sgu_mixer",
    )(xs, w_in, ln_g, ln_b, ws, bst, w_out, g, b, rwt, rb)


def _positions_kernel(cls_ref, pos_ref, tile_ea_ref, tile_eb_ref, nvalid_ref, nused_ref,
                      incl_ref, *, tm):
    t = cls_ref.shape[1]
    ntp = tile_ea_ref.shape[1]
    lanes = 256
    cls = cls_ref[...]
    crow = lax.broadcasted_iota(jnp.int32, (CLASS_ROWS, t), 0)
    onehot = (crow == cls).astype(F32)
    k_i = lax.broadcasted_iota(jnp.int32, (lanes, lanes), 0)
    j_i = lax.broadcasted_iota(jnp.int32, (lanes, lanes), 1)
    upper = (k_i <= j_i).astype(BF16)
    count = jnp.zeros((CLASS_ROWS, 1), F32)
    for c in range(t // lanes):
        sl = slice(c * lanes, (c + 1) * lanes)
        inc = jnp.dot(onehot[:, sl].astype(BF16), upper, preferred_element_type=F32) + count
        incl_ref[:, sl] = inc
        count = inc[:, lanes - 1:lanes]
    rank = jnp.sum(onehot * incl_ref[...], axis=0, keepdims=True) - 1.0
    ntile = jnp.floor((count + (tm - 1)) * (1.0 / tm))
    srow = lax.broadcasted_iota(jnp.int32, (CLASS_ROWS, 1), 0)
    tstart = jnp.zeros((CLASS_ROWS, 1), F32)
    for c in range(N_CLASSES):
        tstart = tstart + jnp.where(srow > c, ntile[c:c + 1, :], 0.0)
    tend = tstart + ntile
    pos = jnp.sum(onehot * (tstart * tm), axis=0, keepdims=True) + rank
    pos_ref[...] = pos.astype(jnp.int32)
    nused = jnp.max(jnp.where(srow < N_CLASSES, tend, 0.0), axis=0, keepdims=True)
    last_cls = jnp.max(jnp.where((ntile > 0) & (srow < N_CLASSES), srow, 0), axis=0, keepdims=True)
    tile_i = lax.broadcasted_iota(jnp.int32, (CLASS_ROWS, ntp), 1).astype(F32)
    is_cls = lax.broadcasted_iota(jnp.int32, (CLASS_ROWS, ntp), 0) < N_CLASSES
    done = (tile_i >= tend) & is_cls
    tcls = jnp.minimum(jnp.sum(done.astype(jnp.int32), axis=0, keepdims=True), last_cls)
    ea = jnp.zeros((1, ntp), jnp.int32)
    eb = jnp.zeros((1, ntp), jnp.int32)
    for c in range(N_CLASSES):
        ea = jnp.where(tcls == c, _CLASS_EA[c], ea)
        eb = jnp.where(tcls == c, _CLASS_EB[c], eb)
    tile_ea_ref[...] = ea
    tile_eb_ref[...] = eb
    inside = (tile_i >= tstart) & (tile_i < tend) & is_cls
    left = jnp.minimum(count - (tile_i - tstart) * tm, float(tm))
    nvalid_ref[...] = jnp.sum(jnp.where(inside, left, 0.0), axis=0, keepdims=True).astype(jnp.int32)
    nused_ref[...] = jnp.broadcast_to(nused, nused_ref.shape).astype(jnp.int32)


def _positions(cls, tm, ntp):
    t = cls.shape[1]
    return pl.pallas_call(
        functools.partial(_positions_kernel, tm=tm),
        out_shape=[
            jax.ShapeDtypeStruct((1, t), jnp.int32),
            jax.ShapeDtypeStruct((1, ntp), jnp.int32),
            jax.ShapeDtypeStruct((1, ntp), jnp.int32),
            jax.ShapeDtypeStruct((1, ntp), jnp.int32),
            jax.ShapeDtypeStruct((1, 128), jnp.int32),
        ],
        scratch_shapes=[pltpu.VMEM((CLASS_ROWS, t), F32)],
        compiler_params=pltpu.CompilerParams(vmem_limit_bytes=V7X_VMEM_LIMIT),
        name="positions",
    )(cls)


def _invert_kernel(pos_ref, tok_ref):
    t = pos_ref.shape[0]
    r = tok_ref.shape[0]

    def fill(j, carry):
        for u in range(INVERT_UNROLL):
            tok_ref[j * INVERT_UNROLL + u] = 0
        return carry

    lax.fori_loop(0, r // INVERT_UNROLL, fill, 0)

    def put(j, carry):
        k0 = j * INVERT_UNROLL
        dst = [pos_ref[k0 + u] for u in range(INVERT_UNROLL)]
        for u in range(INVERT_UNROLL):
            tok_ref[dst[u]] = k0 + u
        return carry

    lax.fori_loop(0, t // INVERT_UNROLL, put, 0)


def _invert(pos, r):
    return pl.pallas_call(
        _invert_kernel,
        in_specs=[pl.BlockSpec(memory_space=pltpu.SMEM)],
        out_specs=pl.BlockSpec(memory_space=pltpu.SMEM),
        out_shape=jax.ShapeDtypeStruct((r,), jnp.int32),
        name="invert_positions",
    )(pos)


def _for_rows(n, per_row, per_group=None):
    ngroups = n // ROW_UNROLL

    def group(j, carry):
        if per_group is not None:
            per_group()
        else:
            for u in range(ROW_UNROLL):
                per_row(j * ROW_UNROLL + u)
        return carry

    lax.fori_loop(0, ngroups, group, 0)

    def single(r, carry):
        per_row(r)
        return carry

    lax.fori_loop(ngroups * ROW_UNROLL, n, single, 0)


def _moe_kernel(tok_ref, ea_ref, eb_ref, nvalid_ref, nused_ref, x_hbm, rw_ref, wga_ref, wua_ref,
                wda_ref, wgb_ref, wub_ref, wdb_ref, g_ref, b_ref, out_hbm, gbuf, gsem, obuf, osem,
                pre_ref, *, tm, out_slab):
    i = pl.program_id(0)
    nused = nused_ref[0]
    slot = lax.rem(i, 2)
    other = 1 - slot
    slab_rows = tm * SUBLANES

    def gather_copy(tile, sl, r):
        dst = pl.multiple_of((sl * tm + r) * SUBLANES, SUBLANES)
        return pltpu.make_async_copy(x_hbm.at[tok_ref[tile * tm + r]],
                                     gbuf.at[pl.ds(dst, SUBLANES), :], gsem.at[sl])

    def start_gather(tile, sl):
        _for_rows(nvalid_ref[tile], lambda r: gather_copy(tile, sl, r).start())

    def wait_gather(tile, sl):
        group = pltpu.make_async_copy(gbuf.at[pl.ds(0, ROW_UNROLL * SUBLANES), :],
                                      gbuf.at[pl.ds(0, ROW_UNROLL * SUBLANES), :], gsem.at[sl])
        _for_rows(nvalid_ref[tile], lambda r: gather_copy(tile, sl, r).wait(), group.wait)

    def scatter_copy(tile, sl, r):
        tok = tok_ref[tile * tm + r]
        if out_slab:
            src = pl.multiple_of((sl * tm + r) * SUBLANES, SUBLANES)
            return pltpu.make_async_copy(obuf.at[pl.ds(src, SUBLANES), :], out_hbm.at[tok],
                                         osem.at[sl])
        return pltpu.make_async_copy(obuf.at[sl, pl.ds(r, 1), :], out_hbm.at[pl.ds(tok, 1), :],
                                     osem.at[sl])

    def start_scatter(tile, sl):
        _for_rows(nvalid_ref[tile], lambda r: scatter_copy(tile, sl, r).start())

    def wait_scatter(tile, sl):
        if out_slab:
            part = obuf.at[pl.ds(0, ROW_UNROLL * SUBLANES), :]
        else:
            part = obuf.at[sl, pl.ds(0, ROW_UNROLL), :]
        group = pltpu.make_async_copy(part, part, osem.at[sl])
        _for_rows(nvalid_ref[tile], lambda r: scatter_copy(tile, sl, r).wait(), group.wait)

    def finish_rows(r0, nrows):
        y = _layer_norm(pre_ref[r0:r0 + nrows, :], g_ref[...], b_ref[...])
        if out_slab:
            _slab_store(obuf, pl.multiple_of(other * slab_rows + r0 * SUBLANES, SUBLANES), y)
        else:
            obuf[other, r0:r0 + nrows, :] = y

    @pl.when(i == 0)
    def _():
        gbuf[...] = jnp.zeros_like(gbuf)
        pre_ref[...] = jnp.zeros_like(pre_ref)

    @pl.when((i == 0) & (nused > 0))
    def _():
        start_gather(0, 0)

    @pl.when(i + 1 < nused)
    def _():
        start_gather(i + 1, other)

    @pl.when((i >= 3) & (i <= nused))
    def _():
        wait_scatter(i - 3, other)

    @pl.when(i < nused)
    def _():
        wait_gather(i, slot)
        x = _slab_load(gbuf, pl.multiple_of(slot * slab_rows, SUBLANES), tm)
        xb = x.astype(BF16)
        half = tm // 2
        ra = rw_ref[pl.ds(ea_ref[i], 1), :]
        rb = rw_ref[pl.ds(eb_ref[i], 1), :]
        sa = jax.nn.sigmoid(jnp.sum(x * ra, axis=-1, keepdims=True))
        sb = jax.nn.sigmoid(jnp.sum(x * rb, axis=-1, keepdims=True))
        denom = sa + sb
        gt = jnp.dot(xb, wga_ref[...].astype(BF16), preferred_element_type=F32)
        finish_rows(0, half)
        up = jnp.dot(xb, wua_ref[...].astype(BF16), preferred_element_type=F32)
        finish_rows(half, half)
        hid = (jax.nn.silu(gt) * up).astype(BF16)
        f = (sa / denom) * jnp.dot(hid, wda_ref[...].astype(BF16), preferred_element_type=F32)
        gt = jnp.dot(xb, wgb_ref[...].astype(BF16), preferred_element_type=F32)
        up = jnp.dot(xb, wub_ref[...].astype(BF16), preferred_element_type=F32)
        hid = (jax.nn.silu(gt) * up).astype(BF16)
        f = f + (sb / denom) * jnp.dot(hid, wdb_ref[...].astype(BF16), preferred_element_type=F32)
        pre_ref[...] = DEEPNORM_ALPHA * x + f

    @pl.when(i == nused)
    def _():
        finish_rows(0, tm)

    @pl.when((i >= 1) & (i <= nused))
    def _():
        start_scatter(i - 1, other)

    @pl.when(i == nused)
    def _():
        for back in (2, 1):
            tile = i - back

            @pl.when(tile >= 0)
            def _():
                wait_scatter(tile, lax.rem(tile, 2))


def _moe(tok, tile_ea, tile_eb, nvalid, nused, x1, rw_rows, wg, wu, wd, g, b, layer, tm, out_slab):
    steps = tile_ea.shape[0]
    t = x1.shape[0]
    d = SUBLANES * LANES
    ff = wg.shape[-1]
    const = lambda shape: pl.BlockSpec(shape, lambda i, *_: (0,) * len(shape))
    up_a = pl.BlockSpec((None, None, d, ff), lambda i, tok, ea, *_: (layer, ea[i], 0, 0))
    up_b = pl.BlockSpec((None, None, d, ff), lambda i, tok, ea, eb, *_: (layer, eb[i], 0, 0))
    dn_a = pl.BlockSpec((None, None, ff, d), lambda i, tok, ea, *_: (layer, ea[i], 0, 0))
    dn_b = pl.BlockSpec((None, None, ff, d), lambda i, tok, ea, eb, *_: (layer, eb[i], 0, 0))
    if out_slab:
        out_shape = jax.ShapeDtypeStruct((t, SUBLANES, LANES), F32)
        obuf = pltpu.VMEM((2 * tm * SUBLANES, LANES), F32)
    else:
        out_shape = jax.ShapeDtypeStruct((t, d), F32)
        obuf = pltpu.VMEM((2, tm, d), F32)
    return pl.pallas_call(
        functools.partial(_moe_kernel, tm=tm, out_slab=out_slab),
        grid_spec=pltpu.PrefetchScalarGridSpec(
            num_scalar_prefetch=5,
            grid=(steps,),
            in_specs=[
                pl.BlockSpec(memory_space=pl.ANY), const((N_EXPERTS, d)),
                up_a, up_a, dn_a, up_b, up_b, dn_b,
                const((1, d)), const((1, d)),
            ],
            out_specs=pl.BlockSpec(memory_space=pl.ANY),
            scratch_shapes=[
                pltpu.VMEM((2 * tm * SUBLANES, LANES), F32), pltpu.SemaphoreType.DMA((2,)),
                obuf, pltpu.SemaphoreType.DMA((2,)),
                pltpu.VMEM((tm, d), F32),
            ],
        ),
        out_shape=out_shape,
        compiler_params=pltpu.CompilerParams(
            dimension_semantics=("arbitrary",), vmem_limit_bytes=V7X_VMEM_LIMIT),
        name=f"moe_{layer}",
    )(tok, tile_ea, tile_eb, nvalid, nused, x1, rw_rows, wg, wu, wd, wg, wu, wd, g, b)


def _route_and_experts(x1_slab, cls, rw_rows, wg, wu, wd, g, b, layer, tm, out_slab):
    t = cls.shape[1]
    nt = t // tm + N_CLASSES
    ntp = -(-(nt + 1) // 128) * 128
    pos, tile_ea, tile_eb, nvalid, nused = _positions(cls, tm, ntp)
    tok = _invert(pos.reshape(t), nt * tm)
    per_step = lambda a: a.reshape(ntp)[:nt + 1]
    return _moe(tok, per_step(tile_ea), per_step(tile_eb), per_step(nvalid), nused.reshape(128)[:1],
                x1_slab.reshape(t, SUBLANES, LANES), rw_rows, wg, wu, wd, g, b, layer, tm, out_slab)


def _forward(x, a_w_in, a_conv_w, a_w_out, b_w_in, b_ln_g, b_ln_b, b_ws, b_bs, b_w_out,
             router_w, router_bias, moe_w_gate, moe_w_up, moe_w_down,
             ln_mix_g, ln_mix_b, ln_ffn_g, ln_ffn_b, *, ts_conv, ts_sgu, tm):
    bn, s, d = x.shape
    assert d == SUBLANES * LANES
    t = bn * s
    rwt = jnp.concatenate(_split_bf16(router_w.T.astype(F32)), axis=0)
    rb = router_bias.astype(F32).reshape(N_EXPERTS, 1)
    rw_rows = router_w.T.astype(F32)
    wg, wu, wd = moe_w_gate, moe_w_up, moe_w_down
    row = lambda a: a.reshape(1, -1)

    x1, cls = _conv_mixer(x, a_w_in[0].astype(BF16), a_conv_w[0], a_w_out[0].astype(BF16),
                          row(ln_mix_g[0]), row(ln_mix_b[0]), rwt, rb, ts_conv)
    x2 = _route_and_experts(x1, cls, rw_rows, wg, wu, wd,
                            row(ln_ffn_g[0]), row(ln_ffn_b[0]), 0, tm, True)
    x3, cls = _sgu_mixer(x2.reshape(t * SUBLANES, LANES), b_w_in[0].astype(BF16),
                         row(b_ln_g[0]), row(b_ln_b[0]), b_ws[0], b_bs[0].T,
                         b_w_out[0].astype(BF16), row(ln_mix_g[1]), row(ln_mix_b[1]), rwt, rb,
                         ts_sgu)
    x4 = _route_and_experts(x3, cls, rw_rows, wg, wu, wd,
                            row(ln_ffn_g[1]), row(ln_ffn_b[1]), 1, tm, False)
    return x4.reshape(bn, s, d)


def kernel(x, a_w_in, a_conv_w, a_w_out, b_w_in, b_ln_g, b_ln_b, b_ws, b_bs, b_w_out, router_w, router_bias, moe_w_gate, moe_w_up, moe_w_down, ln_mix_g, ln_mix_b, ln_ffn_g, ln_ffn_b):
    return _forward(x, a_w_in, a_conv_w, a_w_out, b_w_in, b_ln_g, b_ln_b, b_ws, b_bs, b_w_out,
                    router_w, router_bias, moe_w_gate, moe_w_up, moe_w_down,
                    ln_mix_g, ln_mix_b, ln_ffn_g, ln_ffn_b,
                    ts_conv=512, ts_sgu=512, tm=256)
```

```python
import functools

import jax
import jax.numpy as jnp
from jax import lax
from jax.experimental import pallas as pl
from jax.experimental.pallas import tpu as pltpu

F32 = jnp.float32
BF16 = jnp.bfloat16

N_EXPERTS = 16
N_GROUPS = 4
EXPERTS_PER_GROUP = 4
CHUNK = 128
SGU_HEADS = 8
CONV_WIDTH = 3
DEPTH = 2
DEEPNORM_ALPHA = (2 * DEPTH) ** 0.25
LN_EPS = 1e-5

_PAIRS = ((0, 1), (0, 2), (0, 3), (1, 3), (2, 3), (2, 1))
N_PAIRS = len(_PAIRS)
N_CLASSES = N_GROUPS * N_PAIRS
CLASS_ROWS = 32
_CLASS_EA = tuple(g * EXPERTS_PER_GROUP + p[0] for g in range(N_GROUPS) for p in _PAIRS)
_CLASS_EB = tuple(g * EXPERTS_PER_GROUP + p[1] for g in range(N_GROUPS) for p in _PAIRS)

LANES = 128
SUBLANES = 8
V7X_VMEM_LIMIT = 56 * 1024 * 1024
ROW_UNROLL = 16
INVERT_UNROLL = 16


def _layer_norm(y, g, b):
    mu = jnp.mean(y, axis=-1, keepdims=True)
    d = y - mu
    var = jnp.mean(d * d, axis=-1, keepdims=True)
    return d * lax.rsqrt(var + LN_EPS) * g + b


def _gelu(x):
    return 0.5 * x * (1.0 + lax.erf(x * (2.0 ** -0.5)))


def _slab_load(ref, base, rows):
    return jnp.concatenate(
        [ref[pl.ds(base + s, rows, stride=SUBLANES), :] for s in range(SUBLANES)], axis=1)


def _slab_store(ref, base, val):
    rows = val.shape[0]
    for s in range(SUBLANES):
        ref[pl.ds(base + s, rows, stride=SUBLANES), :] = val[:, s * LANES:(s + 1) * LANES]


def _split_bf16(a):
    hi = a.astype(BF16)
    return hi, (a - hi.astype(F32)).astype(BF16)


def _route_class(x1, rw_split, rb):
    x_hi, x_lo = _split_bf16(x1)
    nt_dims = (((1,), (1,)), ((), ()))
    by_hi = lax.dot_general(rw_split, x_hi, nt_dims, preferred_element_type=F32)
    by_lo = lax.dot_general(rw_split[:N_EXPERTS, :], x_lo, nt_dims, preferred_element_type=F32)
    logits = by_hi[:N_EXPERTS, :] + (by_hi[N_EXPERTS:, :] + by_lo)
    biased = jax.nn.sigmoid(logits) + rb
    v = [biased[e:e + 1, :] for e in range(N_EXPERTS)]
    best_score = None
    best_cls = None
    for g in range(N_GROUPS):
        vg = v[g * EXPERTS_PER_GROUP:(g + 1) * EXPERTS_PER_GROUP]
        sel = []
        for i in range(EXPERTS_PER_GROUP):
            rank = jnp.zeros_like(vg[i], dtype=jnp.int32)
            for j in range(EXPERTS_PER_GROUP):
                if j == i:
                    continue
                beats = (vg[j] >= vg[i]) if j < i else (vg[j] > vg[i])
                rank = rank + beats.astype(jnp.int32)
            sel.append(rank < 2)
        top_sum = None
        cls_g = jnp.zeros_like(vg[0], dtype=jnp.int32)
        for p, (a, b) in enumerate(_PAIRS):
            is_pair = sel[a] & sel[b]
            cls_g = jnp.where(is_pair, g * N_PAIRS + p, cls_g)
            pair_sum = vg[min(a, b)] + vg[max(a, b)]
            top_sum = jnp.where(is_pair, pair_sum, 0.0 if top_sum is None else top_sum)
        if best_score is None:
            best_score, best_cls = top_sum, cls_g
        else:
            better = top_sum > best_score
            best_score = jnp.where(better, top_sum, best_score)
            best_cls = jnp.where(better, cls_g, best_cls)
    return best_cls


def _finish_rows(pre_ref, g_ref, b_ref, rwt_ref, rb_ref, x1_ref, cls_ref, r0, nrows):
    x1 = _layer_norm(pre_ref[r0:r0 + nrows, :], g_ref[...], b_ref[...])
    _slab_store(x1_ref, r0 * SUBLANES, x1)
    cls_ref[:, r0:r0 + nrows] = _route_class(x1, rwt_ref[...], rb_ref[...])


def _conv_mixer_kernel(x_ref, w_in_ref, cw_ref, w_out_ref, g_ref, b_ref, rwt_ref, rb_ref,
                       x1_ref, cls_ref, carry_ref, pre_ref, *, tiles_per_seq):
    ts, d = x_ref.shape
    s = pl.program_id(0)

    @pl.when(s == 0)
    def _():
        pre_ref[...] = jnp.zeros_like(pre_ref)

    @pl.when(lax.rem(s, tiles_per_seq) == 0)
    def _():
        carry_ref[...] = jnp.zeros_like(carry_ref)

    x = x_ref[...]
    xb = x.astype(BF16)
    n_chunks = 2
    piece = 2 * d // n_chunks
    rows_per = ts // n_chunks
    parts = []
    for k in range(n_chunks):
        parts.append(jnp.dot(xb, w_in_ref[:, d + k * piece:d + (k + 1) * piece],
                             preferred_element_type=F32))
        _finish_rows(pre_ref, g_ref, b_ref, rwt_ref, rb_ref, x1_ref, cls_ref,
                     k * rows_per, rows_per)
    gate_h = jnp.concatenate(parts, axis=1)
    ch = gate_h[:, 0:d] * gate_h[:, d:2 * d]
    prev2 = carry_ref[6:7, :]
    prev1 = carry_ref[7:8, :]
    rows = lax.broadcasted_iota(jnp.int32, (SUBLANES, 1), 0)
    ch1 = pltpu.roll(ch, 1, 0)
    ch2 = pltpu.roll(ch, 2, 0)
    head1 = jnp.where(rows == 0, prev1, ch1[0:SUBLANES, :])
    head2 = jnp.where(rows == 0, prev2, jnp.where(rows == 1, prev1, ch2[0:SUBLANES, :]))
    ch1 = jnp.concatenate([head1, ch1[SUBLANES:, :]], axis=0)
    ch2 = jnp.concatenate([head2, ch2[SUBLANES:, :]], axis=0)
    z = cw_ref[0:1, :] * ch2 + cw_ref[1:2, :] * ch1 + cw_ref[2:3, :] * ch
    carry_ref[...] = ch[ts - 8:ts, :]
    bg = jnp.dot(xb, w_in_ref[:, 0:d], preferred_element_type=F32)
    m = jnp.dot((bg * z).astype(BF16), w_out_ref[...], preferred_element_type=F32)
    pre_ref[...] = DEEPNORM_ALPHA * x + m


def _conv_mixer(x, w_in, conv_w, w_out, ln_g, ln_b, rwt, rb, ts):
    bn, s, d = x.shape
    ns = s // ts
    n = bn * ns
    const = lambda shape: pl.BlockSpec(shape, lambda i: (0,) * len(shape))
    cur = lambda i: jnp.minimum(i, n - 1)
    prev = lambda i: jnp.maximum(i - 1, 0)
    return pl.pallas_call(
        functools.partial(_conv_mixer_kernel, tiles_per_seq=ns),
        grid=(n + 1,),
        in_specs=[
            pl.BlockSpec((None, ts, d), lambda i: (cur(i) // ns, cur(i) % ns, 0)),
            const((d, 3 * d)), const((CONV_WIDTH, d)), const((d, d)),
            const((1, d)), const((1, d)), const((2 * N_EXPERTS, d)), const((N_EXPERTS, 1)),
        ],
        out_specs=[
            pl.BlockSpec((ts * SUBLANES, LANES), lambda i: (prev(i), 0)),
            pl.BlockSpec((1, ts), lambda i: (0, prev(i))),
        ],
        out_shape=[
            jax.ShapeDtypeStruct((bn * s * SUBLANES, LANES), F32),
            jax.ShapeDtypeStruct((1, bn * s), jnp.int32),
        ],
        scratch_shapes=[pltpu.VMEM((8, d), F32), pltpu.VMEM((ts, d), F32)],
        compiler_params=pltpu.CompilerParams(
            dimension_semantics=("arbitrary",), vmem_limit_bytes=V7X_VMEM_LIMIT),
        name="conv_mixer",
    )(x, w_in, conv_w, w_out, ln_g, ln_b, rwt, rb)


def _sgu_mixer_kernel(xs_ref, w_in_ref, lng_ref, lnb_ref, ws_ref, bst_ref, w_out_ref,
                      g_ref, b_ref, rwt_ref, rb_ref, x1_ref, cls_ref, v_ref, gate_ref, pre_ref):
    ts, width = v_ref.shape
    hd = width // SGU_HEADS

    @pl.when(pl.program_id(0) == 0)
    def _():
        pre_ref[...] = jnp.zeros_like(pre_ref)

    x = _slab_load(xs_ref, 0, ts)
    xb = x.astype(BF16)
    half = width // 2
    v_lo = jnp.dot(xb, w_in_ref[:, width:width + half], preferred_element_type=F32)
    _finish_rows(pre_ref, g_ref, b_ref, rwt_ref, rb_ref, x1_ref, cls_ref, 0, ts // 2)
    v_hi = jnp.dot(xb, w_in_ref[:, width + half:2 * width], preferred_element_type=F32)
    _finish_rows(pre_ref, g_ref, b_ref, rwt_ref, rb_ref, x1_ref, cls_ref, ts // 2, ts // 2)
    v = _gelu(jnp.concatenate([v_lo, v_hi], axis=1))
    v_ref[...] = _layer_norm(v, lng_ref[...], lnb_ref[...]).astype(BF16)
    r_i = lax.broadcasted_iota(jnp.int32, (CHUNK, CHUNK), 0)
    c_i = lax.broadcasted_iota(jnp.int32, (CHUNK, CHUNK), 1)
    causal = r_i >= c_i
    for h in range(SGU_HEADS):
        cols = slice(h * hd, (h + 1) * hd)
        u_h = _gelu(jnp.dot(xb, w_in_ref[:, cols], preferred_element_type=F32))
        w_h = jnp.where(causal, ws_ref[h], 0.0).astype(BF16)
        bias = bst_ref[:, h:h + 1]
        for c in range(ts // CHUNK):
            rws = slice(c * CHUNK, (c + 1) * CHUNK)
            mixed = jnp.dot(w_h, v_ref[rws, cols], preferred_element_type=F32) + bias
            gate_ref[rws, cols] = (u_h[rws, :] * mixed).astype(BF16)
    m = jnp.dot(gate_ref[...], w_out_ref[...], preferred_element_type=F32)
    pre_ref[...] = DEEPNORM_ALPHA * x + m


def _sgu_mixer(xs, w_in, ln_g, ln_b, ws, bst, w_out, g, b, rwt, rb, ts):
    t = xs.shape[0] // SUBLANES
    n = t // ts
    width, d = w_out.shape
    const = lambda shape: pl.BlockSpec(shape, lambda i: (0,) * len(shape))
    cur = lambda i: jnp.minimum(i, n - 1)
    prev = lambda i: jnp.maximum(i - 1, 0)
    return pl.pallas_call(
        _sgu_mixer_kernel,
        grid=(n + 1,),
        in_specs=[
            pl.BlockSpec((ts * SUBLANES, LANES), lambda i: (cur(i), 0)),
            const((d, 2 * width)), const((1, width)), const((1, width)),
            const((SGU_HEADS, CHUNK, CHUNK)), const((CHUNK, SGU_HEADS)), const((width, d)),
            const((1, d)), const((1, d)), const((2 * N_EXPERTS, d)), const((N_EXPERTS, 1)),
        ],
        out_specs=[
            pl.BlockSpec((ts * SUBLANES, LANES), lambda i: (prev(i), 0)),
            pl.BlockSpec((1, ts), lambda i: (0, prev(i))),
        ],
        out_shape=[
            jax.ShapeDtypeStruct((t * SUBLANES, LANES), F32),
            jax.ShapeDtypeStruct((1, t), jnp.int32),
        ],
        scratch_shapes=[pltpu.VMEM((ts, width), BF16), pltpu.VMEM((ts, width), BF16),
                        pltpu.VMEM((ts, d), F32)],
        compiler_params=pltpu.CompilerParams(
            dimension_semantics=("arbitrary",), vmem_limit_bytes=V7X_VMEM_LIMIT),
        name="sgu_mixer",
    )(xs, w_in, ln_g, ln_b, ws, bst, w_out, g, b, rwt, rb)


def _positions_kernel(cls_ref, pos_ref, tile_ea_ref, tile_eb_ref, nvalid_ref, nused_ref,
                      incl_ref, *, tm):
    t = cls_ref.shape[1]
    ntp = tile_ea_ref.shape[1]
    lanes = 256
    cls = cls_ref[...]
    crow = lax.broadcasted_iota(jnp.int32, (CLASS_ROWS, t), 0)
    onehot = (crow == cls).astype(F32)
    k_i = lax.broadcasted_iota(jnp.int32, (lanes, lanes), 0)
    j_i = lax.broadcasted_iota(jnp.int32, (lanes, lanes), 1)
    upper = (k_i <= j_i).astype(BF16)
    count = jnp.zeros((CLASS_ROWS, 1), F32)
    for c in range(t // lanes):
        sl = slice(c * lanes, (c + 1) * lanes)
        inc = jnp.dot(onehot[:, sl].astype(BF16), upper, preferred_element_type=F32) + count
        incl_ref[:, sl] = inc
        count = inc[:, lanes - 1:lanes]
    rank = jnp.sum(onehot * incl_ref[...], axis=0, keepdims=True) - 1.0
    ntile = jnp.floor((count + (tm - 1)) * (1.0 / tm))
    srow = lax.broadcasted_iota(jnp.int32, (CLASS_ROWS, 1), 0)
    tstart = jnp.zeros((CLASS_ROWS, 1), F32)
    for c in range(N_CLASSES):
        tstart = tstart + jnp.where(srow > c, ntile[c:c + 1, :], 0.0)
    tend = tstart + ntile
    pos = jnp.sum(onehot * (tstart * tm), axis=0, keepdims=True) + rank
    pos_ref[...] = pos.astype(jnp.int32)
    nused = jnp.max(jnp.where(srow < N_CLASSES, tend, 0.0), axis=0, keepdims=True)
    last_cls = jnp.max(jnp.where((ntile > 0) & (srow < N_CLASSES), srow, 0), axis=0, keepdims=True)
    tile_i = lax.broadcasted_iota(jnp.int32, (CLASS_ROWS, ntp), 1).astype(F32)
    is_cls = lax.broadcasted_iota(jnp.int32, (CLASS_ROWS, ntp), 0) < N_CLASSES
    done = (tile_i >= tend) & is_cls
    tcls = jnp.minimum(jnp.sum(done.astype(jnp.int32), axis=0, keepdims=True), last_cls)
    ea = jnp.zeros((1, ntp), jnp.int32)
    eb = jnp.zeros((1, ntp), jnp.int32)
    for c in range(N_CLASSES):
        ea = jnp.where(tcls == c, _CLASS_EA[c], ea)
        eb = jnp.where(tcls == c, _CLASS_EB[c], eb)
    tile_ea_ref[...] = ea
    tile_eb_ref[...] = eb
    inside = (tile_i >= tstart) & (tile_i < tend) & is_cls
    left = jnp.minimum(count - (tile_i - tstart) * tm, float(tm))
    nvalid_ref[...] = jnp.sum(jnp.where(inside, left, 0.0), axis=0, keepdims=True).astype(jnp.int32)
    nused_ref[...] = jnp.broadcast_to(nused, nused_ref.shape).astype(jnp.int32)


def _positions(cls, tm, ntp):
    t = cls.shape[1]
    return pl.pallas_call(
        functools.partial(_positions_kernel, tm=tm),
        out_shape=[
            jax.ShapeDtypeStruct((1, t), jnp.int32),
            jax.ShapeDtypeStruct((1, ntp), jnp.int32),
            jax.ShapeDtypeStruct((1, ntp), jnp.int32),
            jax.ShapeDtypeStruct((1, ntp), jnp.int32),
            jax.ShapeDtypeStruct((1, 128), jnp.int32),
        ],
        scratch_shapes=[pltpu.VMEM((CLASS_ROWS, t), F32)],
        compiler_params=pltpu.CompilerParams(vmem_limit_bytes=V7X_VMEM_LIMIT),
        name="positions",
    )(cls)


def _invert_kernel(pos_ref, tok_ref):
    t = pos_ref.shape[0]
    r = tok_ref.shape[0]

    def fill(j, carry):
        for u in range(INVERT_UNROLL):
            tok_ref[j * INVERT_UNROLL + u] = 0
        return carry

    lax.fori_loop(0, r // INVERT_UNROLL, fill, 0)

    def put(j, carry):
        k0 = j * INVERT_UNROLL
        dst = [pos_ref[k0 + u] for u in range(INVERT_UNROLL)]
        for u in range(INVERT_UNROLL):
            tok_ref[dst[u]] = k0 + u
        return carry

    lax.fori_loop(0, t // INVERT_UNROLL, put, 0)


def _invert(pos, r):
    return pl.pallas_call(
        _invert_kernel,
        in_specs=[pl.BlockSpec(memory_space=pltpu.SMEM)],
        out_specs=pl.BlockSpec(memory_space=pltpu.SMEM),
        out_shape=jax.ShapeDtypeStruct((r,), jnp.int32),
        name="invert_positions",
    )(pos)


def _for_rows(n, per_row, per_group=None):
    ngroups = n // ROW_UNROLL

    def group(j, carry):
        if per_group is not None:
            per_group()
        else:
            for u in range(ROW_UNROLL):
                per_row(j * ROW_UNROLL + u)
        return carry

    lax.fori_loop(0, ngroups, group, 0)

    def single(r, carry):
        per_row(r)
        return carry

    lax.fori_loop(ngroups * ROW_UNROLL, n, single, 0)


def _moe_kernel(tok_ref, ea_ref, eb_ref, nvalid_ref, nused_ref, x_hbm, rw_ref, wga_ref, wua_ref,
                wda_ref, wgb_ref, wub_ref, wdb_ref, g_ref, b_ref, out_hbm, gbuf, gsem, obuf, osem,
                pre_ref, *, tm, out_slab):
    i = pl.program_id(0)
    nused = nused_ref[0]
    slot = lax.rem(i, 2)
    other = 1 - slot
    slab_rows = tm * SUBLANES

    def gather_copy(tile, sl, r):
        dst = pl.multiple_of((sl * tm + r) * SUBLANES, SUBLANES)
        return pltpu.make_async_copy(x_hbm.at[tok_ref[tile * tm + r]],
                                     gbuf.at[pl.ds(dst, SUBLANES), :], gsem.at[sl])

    def start_gather(tile, sl):
        _for_rows(nvalid_ref[tile], lambda r: gather_copy(tile, sl, r).start())

    def wait_gather(tile, sl):
        group = pltpu.make_async_copy(gbuf.at[pl.ds(0, ROW_UNROLL * SUBLANES), :],
                                      gbuf.at[pl.ds(0, ROW_UNROLL * SUBLANES), :], gsem.at[sl])
        _for_rows(nvalid_ref[tile], lambda r: gather_copy(tile, sl, r).wait(), group.wait)

    def scatter_copy(tile, sl, r):
        tok = tok_ref[tile * tm + r]
        if out_slab:
            src = pl.multiple_of((sl * tm + r) * SUBLANES, SUBLANES)
            return pltpu.make_async_copy(obuf.at[pl.ds(src, SUBLANES), :], out_hbm.at[tok],
                                         osem.at[sl])
        return pltpu.make_async_copy(obuf.at[sl, pl.ds(r, 1), :], out_hbm.at[pl.ds(tok, 1), :],
                                     osem.at[sl])

    def start_scatter(tile, sl):
        _for_rows(nvalid_ref[tile], lambda r: scatter_copy(tile, sl, r).start())

    def wait_scatter(tile, sl):
        if out_slab:
            part = obuf.at[pl.ds(0, ROW_UNROLL * SUBLANES), :]
        else:
            part = obuf.at[sl, pl.ds(0, ROW_UNROLL), :]
        group = pltpu.make_async_copy(part, part, osem.at[sl])
        _for_rows(nvalid_ref[tile], lambda r: scatter_copy(tile, sl, r).wait(), group.wait)

    def finish_rows(r0, nrows):
        y = _layer_norm(pre_ref[r0:r0 + nrows, :], g_ref[...], b_ref[...])
        if out_slab:
            _slab_store(obuf, pl.multiple_of(other * slab_rows + r0 * SUBLANES, SUBLANES), y)
        else:
            obuf[other, r0:r0 + nrows, :] = y

    @pl.when(i == 0)
    def _():
        gbuf[...] = jnp.zeros_like(gbuf)
        pre_ref[...] = jnp.zeros_like(pre_ref)

    @pl.when((i == 0) & (nused > 0))
    def _():
        start_gather(0, 0)

    @pl.when(i + 1 < nused)
    def _():
        start_gather(i + 1, other)

    @pl.when((i >= 3) & (i <= nused))
    def _():
        wait_scatter(i - 3, other)

    @pl.when(i < nused)
    def _():
        wait_gather(i, slot)
        x = _slab_load(gbuf, pl.multiple_of(slot * slab_rows, SUBLANES), tm)
        xb = x.astype(BF16)
        half = tm // 2
        ra = rw_ref[pl.ds(ea_ref[i], 1), :]
        rb = rw_ref[pl.ds(eb_ref[i], 1), :]
        sa = jax.nn.sigmoid(jnp.sum(x * ra, axis=-1, keepdims=True))
        sb = jax.nn.sigmoid(jnp.sum(x * rb, axis=-1, keepdims=True))
        denom = sa + sb
        gt = jnp.dot(xb, wga_ref[...].astype(BF16), preferred_element_type=F32)
        finish_rows(0, half)
        up = jnp.dot(xb, wua_ref[...].astype(BF16), preferred_element_type=F32)
        finish_rows(half, half)
        hid = (jax.nn.silu(gt) * up).astype(BF16)
        f = (sa / denom) * jnp.dot(hid, wda_ref[...].astype(BF16), preferred_element_type=F32)
        gt = jnp.dot(xb, wgb_ref[...].astype(BF16), preferred_element_type=F32)
        up = jnp.dot(xb, wub_ref[...].astype(BF16), preferred_element_type=F32)
        hid = (jax.nn.silu(gt) * up).astype(BF16)
        f = f + (sb / denom) * jnp.dot(hid, wdb_ref[...].astype(BF16), preferred_element_type=F32)
        pre_ref[...] = DEEPNORM_ALPHA * x + f

    @pl.when(i == nused)
    def _():
        finish_rows(0, tm)

    @pl.when((i >= 1) & (i <= nused))
    def _():
        start_scatter(i - 1, other)

    @pl.when(i == nused)
    def _():
        for back in (2, 1):
            tile = i - back

            @pl.when(tile >= 0)
            def _():
                wait_scatter(tile, lax.rem(tile, 2))


def _moe(tok, tile_ea, tile_eb, nvalid, nused, x1, rw_rows, wg, wu, wd, g, b, layer, tm, out_slab):
    steps = tile_ea.shape[0]
    t = x1.shape[0]
    d = SUBLANES * LANES
    ff = wg.shape[-1]
    const = lambda shape: pl.BlockSpec(shape, lambda i, *_: (0,) * len(shape))
    up_a = pl.BlockSpec((None, None, d, ff), lambda i, tok, ea, *_: (layer, ea[i], 0, 0))
    up_b = pl.BlockSpec((None, None, d, ff), lambda i, tok, ea, eb, *_: (layer, eb[i], 0, 0))
    dn_a = pl.BlockSpec((None, None, ff, d), lambda i, tok, ea, *_: (layer, ea[i], 0, 0))
    dn_b = pl.BlockSpec((None, None, ff, d), lambda i, tok, ea, eb, *_: (layer, eb[i], 0, 0))
    if out_slab:
        out_shape = jax.ShapeDtypeStruct((t, SUBLANES, LANES), F32)
        obuf = pltpu.VMEM((2 * tm * SUBLANES, LANES), F32)
    else:
        out_shape = jax.ShapeDtypeStruct((t, d), F32)
        obuf = pltpu.VMEM((2, tm, d), F32)
    return pl.pallas_call(
        functools.partial(_moe_kernel, tm=tm, out_slab=out_slab),
        grid_spec=pltpu.PrefetchScalarGridSpec(
            num_scalar_prefetch=5,
            grid=(steps,),
            in_specs=[
                pl.BlockSpec(memory_space=pl.ANY), const((N_EXPERTS, d)),
                up_a, up_a, dn_a, up_b, up_b, dn_b,
                const((1, d)), const((1, d)),
            ],
            out_specs=pl.BlockSpec(memory_space=pl.ANY),
            scratch_shapes=[
                pltpu.VMEM((2 * tm * SUBLANES, LANES), F32), pltpu.SemaphoreType.DMA((2,)),
                obuf, pltpu.SemaphoreType.DMA((2,)),
                pltpu.VMEM((tm, d), F32),
            ],
        ),
        out_shape=out_shape,
        compiler_params=pltpu.CompilerParams(
            dimension_semantics=("arbitrary",), vmem_limit_bytes=V7X_VMEM_LIMIT),
        name=f"moe_{layer}",
    )(tok, tile_ea, tile_eb, nvalid, nused, x1, rw_rows, wg, wu, wd, wg, wu, wd, g, b)


def _route_and_experts(x1_slab, cls, rw_rows, wg, wu, wd, g, b, layer, tm, out_slab):
    t = cls.shape[1]
    nt = t // tm + N_CLASSES
    ntp = -(-(nt + 1) // 128) * 128
    pos, tile_ea, tile_eb, nvalid, nused = _positions(cls, tm, ntp)
    tok = _invert(pos.reshape(t), nt * tm)
    per_step = lambda a: a.reshape(ntp)[:nt + 1]
    return _moe(tok, per_step(tile_ea), per_step(tile_eb), per_step(nvalid), nused.reshape(128)[:1],
                x1_slab.reshape(t, SUBLANES, LANES), rw_rows, wg, wu, wd, g, b, layer, tm, out_slab)


def _forward(x, a_w_in, a_conv_w, a_w_out, b_w_in, b_ln_g, b_ln_b, b_ws, b_bs, b_w_out,
             router_w, router_bias, moe_w_gate, moe_w_up, moe_w_down,
             ln_mix_g, ln_mix_b, ln_ffn_g, ln_ffn_b, *, ts_conv, ts_sgu, tm):
    bn, s, d = x.shape
    assert d == SUBLANES * LANES
    t = bn * s
    rwt = jnp.concatenate(_split_bf16(router_w.T.astype(F32)), axis=0)
    rb = router_bias.astype(F32).reshape(N_EXPERTS, 1)
    rw_rows = router_w.T.astype(F32)
    wg, wu, wd = moe_w_gate, moe_w_up, moe_w_down
    row = lambda a: a.reshape(1, -1)

    x1, cls = _conv_mixer(x, a_w_in[0].astype(BF16), a_conv_w[0], a_w_out[0].astype(BF16),
                          row(ln_mix_g[0]), row(ln_mix_b[0]), rwt, rb, ts_conv)
    x2 = _route_and_experts(x1, cls, rw_rows, wg, wu, wd,
                            row(ln_ffn_g[0]), row(ln_ffn_b[0]), 0, tm, True)
    x3, cls = _sgu_mixer(x2.reshape(t * SUBLANES, LANES), b_w_in[0].astype(BF16),
                         row(b_ln_g[0]), row(b_ln_b[0]), b_ws[0], b_bs[0].T,
                         b_w_out[0].astype(BF16), row(ln_mix_g[1]), row(ln_mix_b[1]), rwt, rb,
                         ts_sgu)
    x4 = _route_and_experts(x3, cls, rw_rows, wg, wu, wd,
                            row(ln_ffn_g[1]), row(ln_ffn_b[1]), 1, tm, False)
    return x4.reshape(bn, s, d)


def kernel(x, a_w_in, a_conv_w, a_w_out, b_w_in, b_ln_g, b_ln_b, b_ws, b_bs, b_w_out, router_w, router_bias, moe_w_gate, moe_w_up, moe_w_down, ln_mix_g, ln_mix_b, ln_ffn_g, ln_ffn_b):
    return _forward(x, a_w_in, a_conv_w, a_w_out, b_w_in, b_ln_g, b_ln_b, b_ws, b_bs, b_w_out,
                    router_w, router_bias, moe_w_gate, moe_w_up, moe_w_down,
                    ln_mix_g, ln_mix_b, ln_ffn_g, ln_ffn_b,
                    ts_conv=512, ts_sgu=512, tm=256)
```

```python
import functools

import jax
import jax.numpy as jnp
from jax import lax
from jax.experimental import pallas as pl
from jax.experimental.pallas import tpu as pltpu

F32 = jnp.float32
BF16 = jnp.bfloat16

N_EXPERTS = 16
N_GROUPS = 4
EXPERTS_PER_GROUP = 4
CHUNK = 128
SGU_HEADS = 8
CONV_WIDTH = 3
DEPTH = 2
DEEPNORM_ALPHA = (2 * DEPTH) ** 0.25
LN_EPS = 1e-5

_PAIRS = ((0, 1), (0, 2), (0, 3), (1, 3), (2, 3), (2, 1))
N_PAIRS = len(_PAIRS)
N_CLASSES = N_GROUPS * N_PAIRS
CLASS_ROWS = 32
_CLASS_EA = tuple(g * EXPERTS_PER_GROUP + p[0] for g in range(N_GROUPS) for p in _PAIRS)
_CLASS_EB = tuple(g * EXPERTS_PER_GROUP + p[1] for g in range(N_GROUPS) for p in _PAIRS)

LANES = 128
SUBLANES = 8
V7X_VMEM_LIMIT = 56 * 1024 * 1024
ROW_UNROLL = 16
INVERT_UNROLL = 16


def _layer_norm(y, g, b):
    mu = jnp.mean(y, axis=-1, keepdims=True)
    d = y - mu
    var = jnp.mean(d * d, axis=-1, keepdims=True)
    return d * lax.rsqrt(var + LN_EPS) * g + b


def _gelu(x):
    return 0.5 * x * (1.0 + lax.erf(x * (2.0 ** -0.5)))


def _slab_load(ref, base, rows):
    return jnp.concatenate(
        [ref[pl.ds(base + s, rows, stride=SUBLANES), :] for s in range(SUBLANES)], axis=1)


def _slab_store(ref, base, val):
    rows = val.shape[0]
    for s in range(SUBLANES):
        ref[pl.ds(base + s, rows, stride=SUBLANES), :] = val[:, s * LANES:(s + 1) * LANES]


def _split_bf16(a):
    hi = a.astype(BF16)
    return hi, (a - hi.astype(F32)).astype(BF16)


def _route_class(x1, rw_split, rb):
    x_hi, x_lo = _split_bf16(x1)
    nt_dims = (((1,), (1,)), ((), ()))
    by_hi = lax.dot_general(rw_split, x_hi, nt_dims, preferred_element_type=F32)
    by_lo = lax.dot_general(rw_split[:N_EXPERTS, :], x_lo, nt_dims, preferred_element_type=F32)
    logits = by_hi[:N_EXPERTS, :] + (by_hi[N_EXPERTS:, :] + by_lo)
    biased = jax.nn.sigmoid(logits) + rb
    v = [biased[e:e + 1, :] for e in range(N_EXPERTS)]
    best_score = None
    best_cls = None
    for g in range(N_GROUPS):
        vg = v[g * EXPERTS_PER_GROUP:(g + 1) * EXPERTS_PER_GROUP]
        sel = []
        for i in range(EXPERTS_PER_GROUP):
            rank = jnp.zeros_like(vg[i], dtype=jnp.int32)
            for j in range(EXPERTS_PER_GROUP):
                if j == i:
                    continue
                beats = (vg[j] >= vg[i]) if j < i else (vg[j] > vg[i])
                rank = rank + beats.astype(jnp.int32)
            sel.append(rank < 2)
        top_sum = None
        cls_g = jnp.zeros_like(vg[0], dtype=jnp.int32)
        for p, (a, b) in enumerate(_PAIRS):
            is_pair = sel[a] & sel[b]
            cls_g = jnp.where(is_pair, g * N_PAIRS + p, cls_g)
            pair_sum = vg[min(a, b)] + vg[max(a, b)]
            top_sum = jnp.where(is_pair, pair_sum, 0.0 if top_sum is None else top_sum)
        if best_score is None:
            best_score, best_cls = top_sum, cls_g
        else:
            better = top_sum > best_score
            best_score = jnp.where(better, top_sum, best_score)
            best_cls = jnp.where(better, cls_g, best_cls)
    return best_cls


def _finish_rows(pre_ref, g_ref, b_ref, rwt_ref, rb_ref, x1_ref, cls_ref, r0, nrows):
    x1 = _layer_norm(pre_ref[r0:r0 + nrows, :], g_ref[...], b_ref[...])
    _slab_store(x1_ref, r0 * SUBLANES, x1)
    cls_ref[:, r0:r0 + nrows] = _route_class(x1, rwt_ref[...], rb_ref[...])


def _conv_mixer_kernel(x_ref, w_in_ref, cw_ref, w_out_ref, g_ref, b_ref, rwt_ref, rb_ref,
                       x1_ref, cls_ref, carry_ref, pre_ref, bz_ref, *, tiles_per_seq):
    ts, d = x_ref.shape
    s = pl.program_id(0)

    @pl.when(s == 0)
    def _():
        pre_ref[...] = jnp.zeros_like(pre_ref)

    @pl.when(lax.rem(s, tiles_per_seq) == 0)
    def _():
        carry_ref[...] = jnp.zeros_like(carry_ref)

    x = x_ref[...]
    xb = x.astype(BF16)
    n_blocks = 4
    wb = d // n_blocks
    rows = lax.broadcasted_iota(jnp.int32, (SUBLANES, 1), 0)
    for c in range(n_blocks):
        cols = slice(c * wb, (c + 1) * wb)
        cg = jnp.dot(xb, w_in_ref[:, d + c * wb:d + (c + 1) * wb], preferred_element_type=F32)
        h = jnp.dot(xb, w_in_ref[:, 2 * d + c * wb:2 * d + (c + 1) * wb],
                    preferred_element_type=F32)
        ch = cg * h
        prev2 = carry_ref[6:7, cols]
        prev1 = carry_ref[7:8, cols]
        ch1 = pltpu.roll(ch, 1, 0)
        ch2 = pltpu.roll(ch, 2, 0)
        head1 = jnp.where(rows == 0, prev1, ch1[0:SUBLANES, :])
        head2 = jnp.where(rows == 0, prev2, jnp.where(rows == 1, prev1, ch2[0:SUBLANES, :]))
        ch1 = jnp.concatenate([head1, ch1[SUBLANES:, :]], axis=0)
        ch2 = jnp.concatenate([head2, ch2[SUBLANES:, :]], axis=0)
        z = cw_ref[0:1, cols] * ch2 + cw_ref[1:2, cols] * ch1 + cw_ref[2:3, cols] * ch
        carry_ref[:, cols] = ch[ts - 8:ts, :]
        bg = jnp.dot(xb, w_in_ref[:, cols], preferred_element_type=F32)
        bz_ref[:, cols] = (bg * z).astype(BF16)
        if c < 2:
            _finish_rows(pre_ref, g_ref, b_ref, rwt_ref, rb_ref, x1_ref, cls_ref,
                         c * (ts // 2), ts // 2)
    m = jnp.dot(bz_ref[...], w_out_ref[...], preferred_element_type=F32)
    pre_ref[...] = DEEPNORM_ALPHA * x + m


def _conv_mixer(x, w_in, conv_w, w_out, ln_g, ln_b, rwt, rb, ts):
    bn, s, d = x.shape
    ns = s // ts
    n = bn * ns
    const = lambda shape: pl.BlockSpec(shape, lambda i: (0,) * len(shape))
    cur = lambda i: jnp.minimum(i, n - 1)
    prev = lambda i: jnp.maximum(i - 1, 0)
    return pl.pallas_call(
        functools.partial(_conv_mixer_kernel, tiles_per_seq=ns),
        grid=(n + 1,),
        in_specs=[
            pl.BlockSpec((None, ts, d), lambda i: (cur(i) // ns, cur(i) % ns, 0)),
            const((d, 3 * d)), const((CONV_WIDTH, d)), const((d, d)),
            const((1, d)), const((1, d)), const((2 * N_EXPERTS, d)), const((N_EXPERTS, 1)),
        ],
        out_specs=[
            pl.BlockSpec((ts * SUBLANES, LANES), lambda i: (prev(i), 0)),
            pl.BlockSpec((1, ts), lambda i: (0, prev(i))),
        ],
        out_shape=[
            jax.ShapeDtypeStruct((bn * s * SUBLANES, LANES), F32),
            jax.ShapeDtypeStruct((1, bn * s), jnp.int32),
        ],
        scratch_shapes=[pltpu.VMEM((8, d), F32), pltpu.VMEM((ts, d), F32),
                        pltpu.VMEM((ts, d), BF16)],
        compiler_params=pltpu.CompilerParams(
            dimension_semantics=("arbitrary",), vmem_limit_bytes=V7X_VMEM_LIMIT),
        name="conv_mixer",
    )(x, w_in, conv_w, w_out, ln_g, ln_b, rwt, rb)


def _sgu_mixer_kernel(xs_ref, w_in_ref, lng_ref, lnb_ref, ws_ref, bst_ref, w_out_ref,
                      g_ref, b_ref, rwt_ref, rb_ref, x1_ref, cls_ref, v_ref, gate_ref, pre_ref):
    ts, width = v_ref.shape
    hd = width // SGU_HEADS

    @pl.when(pl.program_id(0) == 0)
    def _():
        pre_ref[...] = jnp.zeros_like(pre_ref)

    x = _slab_load(xs_ref, 0, ts)
    xb = x.astype(BF16)
    half = width // 2
    v_lo = jnp.dot(xb, w_in_ref[:, width:width + half], preferred_element_type=F32)
    _finish_rows(pre_ref, g_ref, b_ref, rwt_ref, rb_ref, x1_ref, cls_ref, 0, ts // 2)
    v_hi = jnp.dot(xb, w_in_ref[:, width + half:2 * width], preferred_element_type=F32)
    _finish_rows(pre_ref, g_ref, b_ref, rwt_ref, rb_ref, x1_ref, cls_ref, ts // 2, ts // 2)
    v = _gelu(jnp.concatenate([v_lo, v_hi], axis=1))
    v_ref[...] = _layer_norm(v, lng_ref[...], lnb_ref[...]).astype(BF16)
    r_i = lax.broadcasted_iota(jnp.int32, (CHUNK, CHUNK), 0)
    c_i = lax.broadcasted_iota(jnp.int32, (CHUNK, CHUNK), 1)
    causal = r_i >= c_i
    for h in range(SGU_HEADS):
        cols = slice(h * hd, (h + 1) * hd)
        u_h = _gelu(jnp.dot(xb, w_in_ref[:, cols], preferred_element_type=F32))
        w_h = jnp.where(causal, ws_ref[h], 0.0).astype(BF16)
        bias = bst_ref[:, h:h + 1]
        for c in range(ts // CHUNK):
            rws = slice(c * CHUNK, (c + 1) * CHUNK)
            mixed = jnp.dot(w_h, v_ref[rws, cols], preferred_element_type=F32) + bias
            gate_ref[rws, cols] = (u_h[rws, :] * mixed).astype(BF16)
    m = jnp.dot(gate_ref[...], w_out_ref[...], preferred_element_type=F32)
    pre_ref[...] = DEEPNORM_ALPHA * x + m


def _sgu_mixer(xs, w_in, ln_g, ln_b, ws, bst, w_out, g, b, rwt, rb, ts):
    t = xs.shape[0] // SUBLANES
    n = t // ts
    width, d = w_out.shape
    const = lambda shape: pl.BlockSpec(shape, lambda i: (0,) * len(shape))
    cur = lambda i: jnp.minimum(i, n - 1)
    prev = lambda i: jnp.maximum(i - 1, 0)
    return pl.pallas_call(
        _sgu_mixer_kernel,
        grid=(n + 1,),
        in_specs=[
            pl.BlockSpec((ts * SUBLANES, LANES), lambda i: (cur(i), 0)),
            const((d, 2 * width)), const((1, width)), const((1, width)),
            const((SGU_HEADS, CHUNK, CHUNK)), const((CHUNK, SGU_HEADS)), const((width, d)),
            const((1, d)), const((1, d)), const((2 * N_EXPERTS, d)), const((N_EXPERTS, 1)),
        ],
        out_specs=[
            pl.BlockSpec((ts * SUBLANES, LANES), lambda i: (prev(i), 0)),
            pl.BlockSpec((1, ts), lambda i: (0, prev(i))),
        ],
        out_shape=[
            jax.ShapeDtypeStruct((t * SUBLANES, LANES), F32),
            jax.ShapeDtypeStruct((1, t), jnp.int32),
        ],
        scratch_shapes=[pltpu.VMEM((ts, width), BF16), pltpu.VMEM((ts, width), BF16),
                        pltpu.VMEM((ts, d), F32)],
        compiler_params=pltpu.CompilerParams(
            dimension_semantics=("arbitrary",), vmem_limit_bytes=V7X_VMEM_LIMIT),
        name="sgu_mixer",
    )(xs, w_in, ln_g, ln_b, ws, bst, w_out, g, b, rwt, rb)


def _positions_kernel(cls_ref, pos_ref, tile_ea_ref, tile_eb_ref, nvalid_ref, nused_ref,
                      incl_ref, *, tm):
    t = cls_ref.shape[1]
    ntp = tile_ea_ref.shape[1]
    lanes = 256
    cls = cls_ref[...]
    crow = lax.broadcasted_iota(jnp.int32, (CLASS_ROWS, t), 0)
    onehot = (crow == cls).astype(F32)
    k_i = lax.broadcasted_iota(jnp.int32, (lanes, lanes), 0)
    j_i = lax.broadcasted_iota(jnp.int32, (lanes, lanes), 1)
    upper = (k_i <= j_i).astype(BF16)
    count = jnp.zeros((CLASS_ROWS, 1), F32)
    for c in range(t // lanes):
        sl = slice(c * lanes, (c + 1) * lanes)
        inc = jnp.dot(onehot[:, sl].astype(BF16), upper, preferred_element_type=F32) + count
        incl_ref[:, sl] = inc
        count = inc[:, lanes - 1:lanes]
    rank = jnp.sum(onehot * incl_ref[...], axis=0, keepdims=True) - 1.0
    ntile = jnp.floor((count + (tm - 1)) * (1.0 / tm))
    srow = lax.broadcasted_iota(jnp.int32, (CLASS_ROWS, 1), 0)
    tstart = jnp.zeros((CLASS_ROWS, 1), F32)
    for c in range(N_CLASSES):
        tstart = tstart + jnp.where(srow > c, ntile[c:c + 1, :], 0.0)
    tend = tstart + ntile
    pos = jnp.sum(onehot * (tstart * tm), axis=0, keepdims=True) + rank
    pos_ref[...] = pos.astype(jnp.int32)
    nused = jnp.max(jnp.where(srow < N_CLASSES, tend, 0.0), axis=0, keepdims=True)
    last_cls = jnp.max(jnp.where((ntile > 0) & (srow < N_CLASSES), srow, 0), axis=0, keepdims=True)
    tile_i = lax.broadcasted_iota(jnp.int32, (CLASS_ROWS, ntp), 1).astype(F32)
    is_cls = lax.broadcasted_iota(jnp.int32, (CLASS_ROWS, ntp), 0) < N_CLASSES
    done = (tile_i >= tend) & is_cls
    tcls = jnp.minimum(jnp.sum(done.astype(jnp.int32), axis=0, keepdims=True), last_cls)
    ea = jnp.zeros((1, ntp), jnp.int32)
    eb = jnp.zeros((1, ntp), jnp.int32)
    for c in range(N_CLASSES):
        ea = jnp.where(tcls == c, _CLASS_EA[c], ea)
        eb = jnp.where(tcls == c, _CLASS_EB[c], eb)
    tile_ea_ref[...] = ea
    tile_eb_ref[...] = eb
    inside = (tile_i >= tstart) & (tile_i < tend) & is_cls
    left = jnp.minimum(count - (tile_i - tstart) * tm, float(tm))
    nvalid_ref[...] = jnp.sum(jnp.where(inside, left, 0.0), axis=0, keepdims=True).astype(jnp.int32)
    nused_ref[...] = jnp.broadcast_to(nused, nused_ref.shape).astype(jnp.int32)


def _positions(cls, tm, ntp):
    t = cls.shape[1]
    return pl.pallas_call(
        functools.partial(_positions_kernel, tm=tm),
        out_shape=[
            jax.ShapeDtypeStruct((1, t), jnp.int32),
            jax.ShapeDtypeStruct((1, ntp), jnp.int32),
            jax.ShapeDtypeStruct((1, ntp), jnp.int32),
            jax.ShapeDtypeStruct((1, ntp), jnp.int32),
            jax.ShapeDtypeStruct((1, 128), jnp.int32),
        ],
        scratch_shapes=[pltpu.VMEM((CLASS_ROWS, t), F32)],
        compiler_params=pltpu.CompilerParams(vmem_limit_bytes=V7X_VMEM_LIMIT),
        name="positions",
    )(cls)


def _invert_kernel(pos_ref, tok_ref):
    t = pos_ref.shape[0]
    r = tok_ref.shape[0]

    def fill(j, carry):
        for u in range(INVERT_UNROLL):
            tok_ref[j * INVERT_UNROLL + u] = 0
        return carry

    lax.fori_loop(0, r // INVERT_UNROLL, fill, 0)

    def put(j, carry):
        k0 = j * INVERT_UNROLL
        dst = [pos_ref[k0 + u] for u in range(INVERT_UNROLL)]
        for u in range(INVERT_UNROLL):
            tok_ref[dst[u]] = k0 + u
        return carry

    lax.fori_loop(0, t // INVERT_UNROLL, put, 0)


def _invert(pos, r):
    return pl.pallas_call(
        _invert_kernel,
        in_specs=[pl.BlockSpec(memory_space=pltpu.SMEM)],
        out_specs=pl.BlockSpec(memory_space=pltpu.SMEM),
        out_shape=jax.ShapeDtypeStruct((r,), jnp.int32),
        name="invert_positions",
    )(pos)


def _for_rows(n, per_row, per_group=None):
    ngroups = n // ROW_UNROLL

    def group(j, carry):
        if per_group is not None:
            per_group()
        else:
            for u in range(ROW_UNROLL):
                per_row(j * ROW_UNROLL + u)
        return carry

    lax.fori_loop(0, ngroups, group, 0)

    def single(r, carry):
        per_row(r)
        return carry

    lax.fori_loop(ngroups * ROW_UNROLL, n, single, 0)


def _moe_kernel(tok_ref, ea_ref, eb_ref, nvalid_ref, nused_ref, x_hbm, rw_ref, wga_ref, wua_ref,
                wda_ref, wgb_ref, wub_ref, wdb_ref, g_ref, b_ref, out_hbm, gbuf, gsem, obuf, osem,
                pre_ref, *, tm, out_slab):
    i = pl.program_id(0)
    nused = nused_ref[0]
    slot = lax.rem(i, 2)
    other = 1 - slot
    slab_rows = tm * SUBLANES

    def gather_copy(tile, sl, r):
        dst = pl.multiple_of((sl * tm + r) * SUBLANES, SUBLANES)
        return pltpu.make_async_copy(x_hbm.at[tok_ref[tile * tm + r]],
                                     gbuf.at[pl.ds(dst, SUBLANES), :], gsem.at[sl])

    def start_gather(tile, sl):
        _for_rows(nvalid_ref[tile], lambda r: gather_copy(tile, sl, r).start())

    def wait_gather(tile, sl):
        group = pltpu.make_async_copy(gbuf.at[pl.ds(0, ROW_UNROLL * SUBLANES), :],
                                      gbuf.at[pl.ds(0, ROW_UNROLL * SUBLANES), :], gsem.at[sl])
        _for_rows(nvalid_ref[tile], lambda r: gather_copy(tile, sl, r).wait(), group.wait)

    def scatter_copy(tile, sl, r):
        tok = tok_ref[tile * tm + r]
        if out_slab:
            src = pl.multiple_of((sl * tm + r) * SUBLANES, SUBLANES)
            return pltpu.make_async_copy(obuf.at[pl.ds(src, SUBLANES), :], out_hbm.at[tok],
                                         osem.at[sl])
        return pltpu.make_async_copy(obuf.at[sl, pl.ds(r, 1), :], out_hbm.at[pl.ds(tok, 1), :],
                                     osem.at[sl])

    def start_scatter(tile, sl):
        _for_rows(nvalid_ref[tile], lambda r: scatter_copy(tile, sl, r).start())

    def wait_scatter(tile, sl):
        if out_slab:
            part = obuf.at[pl.ds(0, ROW_UNROLL * SUBLANES), :]
        else:
            part = obuf.at[sl, pl.ds(0, ROW_UNROLL), :]
        group = pltpu.make_async_copy(part, part, osem.at[sl])
        _for_rows(nvalid_ref[tile], lambda r: scatter_copy(tile, sl, r).wait(), group.wait)

    def finish_rows(r0, nrows):
        y = _layer_norm(pre_ref[r0:r0 + nrows, :], g_ref[...], b_ref[...])
        if out_slab:
            _slab_store(obuf, pl.multiple_of(other * slab_rows + r0 * SUBLANES, SUBLANES), y)
        else:
            obuf[other, r0:r0 + nrows, :] = y

    @pl.when(i == 0)
    def _():
        gbuf[...] = jnp.zeros_like(gbuf)
        pre_ref[...] = jnp.zeros_like(pre_ref)

    @pl.when((i == 0) & (nused > 0))
    def _():
        start_gather(0, 0)

    @pl.when(i + 1 < nused)
    def _():
        start_gather(i + 1, other)

    @pl.when((i >= 3) & (i <= nused))
    def _():
        wait_scatter(i - 3, other)

    @pl.when(i < nused)
    def _():
        wait_gather(i, slot)
        x = _slab_load(gbuf, pl.multiple_of(slot * slab_rows, SUBLANES), tm)
        xb = x.astype(BF16)
        half = tm // 2
        ra = rw_ref[pl.ds(ea_ref[i], 1), :]
        rb = rw_ref[pl.ds(eb_ref[i], 1), :]
        sa = jax.nn.sigmoid(jnp.sum(x * ra, axis=-1, keepdims=True))
        sb = jax.nn.sigmoid(jnp.sum(x * rb, axis=-1, keepdims=True))
        denom = sa + sb
        gt = jnp.dot(xb, wga_ref[...].astype(BF16), preferred_element_type=F32)
        finish_rows(0, half)
        up = jnp.dot(xb, wua_ref[...].astype(BF16), preferred_element_type=F32)
        finish_rows(half, half)
        hid = (jax.nn.silu(gt) * up).astype(BF16)
        f = (sa / denom) * jnp.dot(hid, wda_ref[...].astype(BF16), preferred_element_type=F32)
        gt = jnp.dot(xb, wgb_ref[...].astype(BF16), preferred_element_type=F32)
        up = jnp.dot(xb, wub_ref[...].astype(BF16), preferred_element_type=F32)
        hid = (jax.nn.silu(gt) * up).astype(BF16)
        f = f + (sb / denom) * jnp.dot(hid, wdb_ref[...].astype(BF16), preferred_element_type=F32)
        pre_ref[...] = DEEPNORM_ALPHA * x + f

    @pl.when(i == nused)
    def _():
        finish_rows(0, tm)

    @pl.when((i >= 1) & (i <= nused))
    def _():
        start_scatter(i - 1, other)

    @pl.when(i == nused)
    def _():
        for back in (2, 1):
            tile = i - back

            @pl.when(tile >= 0)
            def _():
                wait_scatter(tile, lax.rem(tile, 2))


def _moe(tok, tile_ea, tile_eb, nvalid, nused, x1, rw_rows, wg, wu, wd, g, b, layer, tm, out_slab):
    steps = tile_ea.shape[0]
    t = x1.shape[0]
    d = SUBLANES * LANES
    ff = wg.shape[-1]
    const = lambda shape: pl.BlockSpec(shape, lambda i, *_: (0,) * len(shape))
    up_a = pl.BlockSpec((None, None, d, ff), lambda i, tok, ea, *_: (layer, ea[i], 0, 0))
    up_b = pl.BlockSpec((None, None, d, ff), lambda i, tok, ea, eb, *_: (layer, eb[i], 0, 0))
    dn_a = pl.BlockSpec((None, None, ff, d), lambda i, tok, ea, *_: (layer, ea[i], 0, 0))
    dn_b = pl.BlockSpec((None, None, ff, d), lambda i, tok, ea, eb, *_: (layer, eb[i], 0, 0))
    if out_slab:
        out_shape = jax.ShapeDtypeStruct((t, SUBLANES, LANES), F32)
        obuf = pltpu.VMEM((2 * tm * SUBLANES, LANES), F32)
    else:
        out_shape = jax.ShapeDtypeStruct((t, d), F32)
        obuf = pltpu.VMEM((2, tm, d), F32)
    return pl.pallas_call(
        functools.partial(_moe_kernel, tm=tm, out_slab=out_slab),
        grid_spec=pltpu.PrefetchScalarGridSpec(
            num_scalar_prefetch=5,
            grid=(steps,),
            in_specs=[
                pl.BlockSpec(memory_space=pl.ANY), const((N_EXPERTS, d)),
                up_a, up_a, dn_a, up_b, up_b, dn_b,
                const((1, d)), const((1, d)),
            ],
            out_specs=pl.BlockSpec(memory_space=pl.ANY),
            scratch_shapes=[
                pltpu.VMEM((2 * tm * SUBLANES, LANES), F32), pltpu.SemaphoreType.DMA((2,)),
                obuf, pltpu.SemaphoreType.DMA((2,)),
                pltpu.VMEM((tm, d), F32),
            ],
        ),
        out_shape=out_shape,
        compiler_params=pltpu.CompilerParams(
            dimension_semantics=("arbitrary",), vmem_limit_bytes=V7X_VMEM_LIMIT),
        name=f"moe_{layer}",
    )(tok, tile_ea, tile_eb, nvalid, nused, x1, rw_rows, wg, wu, wd, wg, wu, wd, g, b)


def _route_and_experts(x1_slab, cls, rw_rows, wg, wu, wd, g, b, layer, tm, out_slab):
    t = cls.shape[1]
    nt = t // tm + N_CLASSES
    ntp = -(-(nt + 1) // 128) * 128
    pos, tile_ea, tile_eb, nvalid, nused = _positions(cls, tm, ntp)
    tok = _invert(pos.reshape(t), nt * tm)
    per_step = lambda a: a.reshape(ntp)[:nt + 1]
    return _moe(tok, per_step(tile_ea), per_step(tile_eb), per_step(nvalid), nused.reshape(128)[:1],
                x1_slab.reshape(t, SUBLANES, LANES), rw_rows, wg, wu, wd, g, b, layer, tm, out_slab)


def _forward(x, a_w_in, a_conv_w, a_w_out, b_w_in, b_ln_g, b_ln_b, b_ws, b_bs, b_w_out,
             router_w, router_bias, moe_w_gate, moe_w_up, moe_w_down,
             ln_mix_g, ln_mix_b, ln_ffn_g, ln_ffn_b, *, ts_conv, ts_sgu, tm):
    bn, s, d = x.shape
    assert d == SUBLANES * LANES
    t = bn * s
    rwt = jnp.concatenate(_split_bf16(router_w.T.astype(F32)), axis=0)
    rb = router_bias.astype(F32).reshape(N_EXPERTS, 1)
    rw_rows = router_w.T.astype(F32)
    wg, wu, wd = moe_w_gate, moe_w_up, moe_w_down
    row = lambda a: a.reshape(1, -1)

    x1, cls = _conv_mixer(x, a_w_in[0].astype(BF16), a_conv_w[0], a_w_out[0].astype(BF16),
                          row(ln_mix_g[0]), row(ln_mix_b[0]), rwt, rb, ts_conv)
    x2 = _route_and_experts(x1, cls, rw_rows, wg, wu, wd,
                            row(ln_ffn_g[0]), row(ln_ffn_b[0]), 0, tm, True)
    x3, cls = _sgu_mixer(x2.reshape(t * SUBLANES, LANES), b_w_in[0].astype(BF16),
                         row(b_ln_g[0]), row(b_ln_b[0]), b_ws[0], b_bs[0].T,
                         b_w_out[0].astype(BF16), row(ln_mix_g[1]), row(ln_mix_b[1]), rwt, rb,
                         ts_sgu)
    x4 = _route_and_experts(x3, cls, rw_rows, wg, wu, wd,
                            row(ln_ffn_g[1]), row(ln_ffn_b[1]), 1, tm, False)
    return x4.reshape(bn, s, d)


def kernel(x, a_w_in, a_conv_w, a_w_out, b_w_in, b_ln_g, b_ln_b, b_ws, b_bs, b_w_out, router_w, router_bias, moe_w_gate, moe_w_up, moe_w_down, ln_mix_g, ln_mix_b, ln_ffn_g, ln_ffn_b):
    return _forward(x, a_w_in, a_conv_w, a_w_out, b_w_in, b_ln_g, b_ln_b, b_ws, b_bs, b_w_out,
                    router_w, router_bias, moe_w_gate, moe_w_up, moe_w_down,
                    ln_mix_g, ln_mix_b, ln_ffn_g, ln_ffn_b,
                    ts_conv=512, ts_sgu=512, tm=256)
```
